```python
import math
import jax, jax.numpy as jnp
from jax import lax
import numpy as np

D_MODEL = 1024
BATCH = 8
SEQ = 2048
DEPTH = 4

MIX_WIDTH = D_MODEL
N_DA_HEADS = 4
DA_HEAD_DIM = 64
DA_V_DIM = 2 * DA_HEAD_DIM
DA_WIDTH = N_DA_HEADS * DA_V_DIM
N_GDN_HEADS = 4
GDN_HEAD_DIM = 128
GDN_WIDTH = N_GDN_HEADS * GDN_HEAD_DIM
CONV_WIDTH = 4
GDN_CHUNK = 64
Q_BLOCK = 128
ROPE_THETA = 10000.0
DA_Q_COLS = N_DA_HEADS * 2 * DA_HEAD_DIM
DA_K_COLS = N_DA_HEADS * 2 * DA_HEAD_DIM
DA_V_COLS = DA_WIDTH
GDN_QKV_COLS = 3 * GDN_WIDTH
GDN_Z_COLS = GDN_WIDTH
GDN_A_COLS = N_GDN_HEADS
GDN_B_COLS = N_GDN_HEADS
D_IN_PROJ = DA_Q_COLS + DA_K_COLS + DA_V_COLS + GDN_QKV_COLS + GDN_Z_COLS + GDN_A_COLS + GDN_B_COLS
D_FF_DENSE = 2816
N_EXPERTS = 8
TOP_K = 2
D_FF_EXPERT = 3584
EXPERT_BLOCK = 256
N_DENSE = (DEPTH + 1) // 2
N_MOE = DEPTH // 2
DEEPNORM_ALPHA = (2.0 * DEPTH) ** 0.25
DEEPNORM_BETA = (8.0 * DEPTH) ** -0.25
LN_EPS = 1e-5
RMS_EPS = 1e-6

kernel_name = "hybrid_diffattn_gdn_deepnorm_moe"


def layer_norm(x, g, b):
    xf = x.astype(jnp.float32)
    mu = xf.mean(-1, keepdims=True)
    var = jnp.square(xf - mu).mean(-1, keepdims=True)
    return ((xf - mu) * lax.rsqrt(var + LN_EPS) * g.astype(jnp.float32) + b.astype(jnp.float32)).astype(x.dtype)


def rms_norm(x, w):
    xf = x.astype(jnp.float32)
    return (xf * lax.rsqrt(jnp.mean(xf * xf, -1, keepdims=True) + RMS_EPS) * w.astype(jnp.float32)).astype(x.dtype)


def l2_normalize(x):
    xf = x.astype(jnp.float32)
    return xf * lax.rsqrt(jnp.sum(xf * xf, -1, keepdims=True) + RMS_EPS)


def apply_rope(x, cos, sin):
    half = x.shape[-1] // 2
    x1, x2 = x[..., :half], x[..., half:]
    rot = jnp.concatenate([-x2, x1], axis=-1)
    return x * cos + rot * sin


def diff_attention(q, k, v, lam, lam_init, subln_w, cos, sin):
    B, S, H = q.shape[0], q.shape[1], q.shape[2]
    nb = S // Q_BLOCK
    q = apply_rope(q, cos, sin) * (DA_HEAD_DIM ** -0.5)
    k = apply_rope(k, cos, sin)
    qb = jnp.moveaxis(q.reshape(B, nb, Q_BLOCK, H, 2, DA_HEAD_DIM), 1, 0)
    k_pos = jnp.arange(S)

    def block(args):
        q_blk, bi = args
        s = jnp.einsum('bqhcd,bkhcd->bhcqk', q_blk, k).astype(jnp.float32)
        q_pos = bi * Q_BLOCK + jnp.arange(Q_BLOCK)
        s = jnp.where(q_pos[:, None] >= k_pos[None, :], s, -jnp.inf)
        p = jax.nn.softmax(s, axis=-1)
        p_diff = (p[:, :, 0] - lam * p[:, :, 1]).astype(v.dtype)
        return jnp.einsum('bhqk,bkhe->bqhe', p_diff, v)

    o = lax.map(block, (qb, jnp.arange(nb)))
    o = jnp.moveaxis(o, 0, 1).reshape(B, S, H, DA_V_DIM)
    o = rms_norm(o, subln_w) * (1.0 - lam_init)
    return o.reshape(B, S, H * DA_V_DIM)


def causal_short_conv(x, w):
    C = x.shape[-1]
    return lax.conv_general_dilated(
        x, w.astype(x.dtype)[:, None, :], window_strides=(1,), padding=[(CONV_WIDTH - 1, 0)],
        dimension_numbers=('NWC', 'WIO', 'NWC'), feature_group_count=C)


def gated_delta_rule_chunked(q, k, v, g, beta):
    B, S, H, Dk = q.shape
    Dv = v.shape[-1]
    C = GDN_CHUNK
    n = S // C

    def chunks(t):
        return jnp.moveaxis(t.reshape(B, n, C, H, *t.shape[3:]), 3, 1)

    q, k, v, g, beta = chunks(q), chunks(k), chunks(v), chunks(g), chunks(beta)
    G = jnp.cumsum(g, axis=-1)
    idx = jnp.arange(C)
    tril = idx[:, None] >= idx[None, :]
    strict = idx[:, None] > idx[None, :]
    decay = jnp.exp(jnp.where(tril, G[..., :, None] - G[..., None, :], -jnp.inf))
    k_beta = k * beta[..., None]
    L = jnp.where(strict, jnp.einsum('bhncd,bhnkd->bhnck', k_beta, k) * decay, 0.0)
    tri = L + jnp.eye(C, dtype=L.dtype)
    rhs = jnp.concatenate([v * beta[..., None], k_beta * jnp.exp(G)[..., None]], axis=-1)
    sol = lax.linalg.triangular_solve(tri, rhs, left_side=True, lower=True, unit_diagonal=True)
    u, w = sol[..., :Dv], sol[..., Dv:]
    qk = jnp.where(tril, jnp.einsum('bhncd,bhnkd->bhnck', q, k) * decay, 0.0)
    q_dec = q * jnp.exp(G)[..., None]
    k_tail = k * jnp.exp(G[..., -1:] - G)[..., None]
    chunk_decay = jnp.exp(G[..., -1])

    def step(state, xs):
        u_c, w_c, qk_c, qd_c, kt_c, cd_c = xs
        v_new = u_c - jnp.einsum('bhcd,bhde->bhce', w_c, state)
        out = jnp.einsum('bhcd,bhde->bhce', qd_c, state) + jnp.einsum('bhck,bhke->bhce', qk_c, v_new)
        state = state * cd_c[..., None, None] + jnp.einsum('bhcd,bhce->bhde', kt_c, v_new)
        return state, out

    xs = tuple(jnp.moveaxis(t, 2, 0) for t in (u, w, qk, q_dec, k_tail, chunk_decay))
    _, o = lax.scan(step, jnp.zeros((B, H, Dk, Dv), jnp.float32), xs)
    return jnp.transpose(o, (1, 0, 3, 2, 4)).reshape(B, S, H, Dv)


def hybrid_mixer(x, w_in, conv_w, a_log, dt_bias, gdn_norm_w, lam_q1, lam_k1, lam_q2, lam_k2,
                 subln_w, w_out, cos, sin, lam_init):
    B, S, _ = x.shape
    proj = jnp.einsum('bsd,de->bse', x, w_in)
    splits = np.cumsum([DA_Q_COLS, DA_K_COLS, DA_V_COLS, GDN_QKV_COLS, GDN_Z_COLS, GDN_A_COLS]).tolist()
    da_q, da_k, da_v, g_qkv, g_z, g_a, g_b = jnp.split(proj, splits, axis=-1)

    lam = (jnp.exp(jnp.sum(lam_q1.astype(jnp.float32) * lam_k1.astype(jnp.float32)))
           - jnp.exp(jnp.sum(lam_q2.astype(jnp.float32) * lam_k2.astype(jnp.float32))) + lam_init)
    da_out = diff_attention(
        da_q.reshape(B, S, N_DA_HEADS, 2, DA_HEAD_DIM), da_k.reshape(B, S, N_DA_HEADS, 2, DA_HEAD_DIM),
        da_v.reshape(B, S, N_DA_HEADS, DA_V_DIM), lam, lam_init, subln_w, cos, sin)

    qkv = jax.nn.silu(causal_short_conv(g_qkv, conv_w))
    gq, gk, gv = jnp.split(qkv, 3, axis=-1)
    gq = l2_normalize(gq.reshape(B, S, N_GDN_HEADS, GDN_HEAD_DIM)) * (GDN_HEAD_DIM ** -0.5)
    gk = l2_normalize(gk.reshape(B, S, N_GDN_HEADS, GDN_HEAD_DIM))
    gv = gv.reshape(B, S, N_GDN_HEADS, GDN_HEAD_DIM).astype(jnp.float32)
    beta = jax.nn.sigmoid(g_b.astype(jnp.float32))
    g = -jnp.exp(a_log.astype(jnp.float32)) * jax.nn.softplus(g_a.astype(jnp.float32) + dt_bias.astype(jnp.float32))
    o = gated_delta_rule_chunked(gq, gk, gv, g, beta).astype(x.dtype)
    o = rms_norm(o, gdn_norm_w) * jax.nn.silu(g_z.reshape(B, S, N_GDN_HEADS, GDN_HEAD_DIM))
    gdn_out = o.reshape(B, S, GDN_WIDTH)

    merged = jnp.concatenate([da_out, gdn_out], axis=-1)
    return jnp.einsum('bse,ed->bsd', merged, w_out)


def swiglu(x, w_gate, w_up, w_down):
    return (jax.nn.silu(x @ w_gate) * (x @ w_up)) @ w_down


def moe_ffn(x, router_w, w_gate, w_up, w_down):
    B, S, D = x.shape
    T = B * S
    xf = x.reshape(T, D)
    logits = jnp.einsum('td,de->te', xf, router_w).astype(jnp.float32)
    top_val, top_idx = lax.top_k(logits, TOP_K)
    gates = jax.nn.softmax(top_val, axis=-1)
    A = T * TOP_K
    e = top_idx.reshape(A)
    tok = jnp.repeat(jnp.arange(T, dtype=jnp.int32), TOP_K)
    wts = gates.reshape(A)
    order = jnp.argsort(e, stable=True)
    e_s, tok_s, w_s = e[order], tok[order], wts[order]
    counts = jnp.bincount(e, length=N_EXPERTS)
    starts = jnp.cumsum(counts) - counts
    padded = ((counts + EXPERT_BLOCK - 1) // EXPERT_BLOCK) * EXPERT_BLOCK
    pend = jnp.cumsum(padded)
    pstarts = pend - padded
    dest = pstarts[e_s] + (jnp.arange(A) - starts[e_s])
    P = A + N_EXPERTS * EXPERT_BLOCK
    n_blk = P // EXPERT_BLOCK
    tok_buf = jnp.zeros((P,), jnp.int32).at[dest].set(tok_s)
    w_buf = jnp.zeros((P,), jnp.float32).at[dest].set(w_s)
    blk_e = jnp.minimum(jnp.searchsorted(pend, jnp.arange(n_blk) * EXPERT_BLOCK, side='right'), N_EXPERTS - 1)

    def expert_block(args):
        tok_b, e_b = args
        xb = xf[tok_b]
        return swiglu(xb, w_gate[e_b], w_up[e_b], w_down[e_b])

    y = lax.map(expert_block, (tok_buf.reshape(n_blk, EXPERT_BLOCK), blk_e)).reshape(P, D)
    y = y * w_buf[:, None].astype(y.dtype)
    out = jax.ops.segment_sum(y, tok_buf, num_segments=T)
    return out.reshape(B, S, D)


def setup_inputs(seed: int = 0) -> dict:
    key = jax.random.key(seed)
    ks = jax.random.split(key, 24)
    f32 = jnp.float32
    nrm = lambda k, shape, s: jax.random.normal(k, shape, f32) * s
    x = jax.random.normal(ks[0], (BATCH, SEQ, D_MODEL), f32)
    positions = jnp.broadcast_to(jnp.arange(SEQ, dtype=jnp.int32), (BATCH, SEQ))
    w_in = nrm(ks[1], (DEPTH, D_MODEL, D_IN_PROJ), D_MODEL ** -0.5)
    conv_w = nrm(ks[2], (DEPTH, CONV_WIDTH, GDN_QKV_COLS), 0.5)
    a_log = jnp.log(jax.random.uniform(ks[3], (DEPTH, N_GDN_HEADS), f32, 1.0, 16.0))
    dt = jnp.exp(jax.random.uniform(ks[4], (DEPTH, N_GDN_HEADS), f32, math.log(1e-3), math.log(1e-1)))
    dt_bias = dt + jnp.log(-jnp.expm1(-dt))
    gdn_norm_w = 1.0 + nrm(ks[5], (DEPTH, GDN_HEAD_DIM), 0.02)
    lam_q1 = nrm(ks[6], (DEPTH, DA_HEAD_DIM), 0.1)
    lam_k1 = nrm(ks[7], (DEPTH, DA_HEAD_DIM), 0.1)
    lam_q2 = nrm(ks[8], (DEPTH, DA_HEAD_DIM), 0.1)
    lam_k2 = nrm(ks[9], (DEPTH, DA_HEAD_DIM), 0.1)
    subln_w = 1.0 + nrm(ks[10], (DEPTH, DA_V_DIM), 0.02)
    w_out = nrm(ks[11], (DEPTH, MIX_WIDTH, D_MODEL), MIX_WIDTH ** -0.5 * DEEPNORM_BETA)
    ln1_g = 1.0 + nrm(ks[12], (DEPTH, D_MODEL), 0.02)
    ln1_b = nrm(ks[13], (DEPTH, D_MODEL), 0.02)
    ln2_g = 1.0 + nrm(ks[14], (DEPTH, D_MODEL), 0.02)
    ln2_b = nrm(ks[15], (DEPTH, D_MODEL), 0.02)
    ffn_w_gate = nrm(ks[16], (N_DENSE, D_MODEL, D_FF_DENSE), D_MODEL ** -0.5)
    ffn_w_up = nrm(ks[17], (N_DENSE, D_MODEL, D_FF_DENSE), D_MODEL ** -0.5)
    ffn_w_down = nrm(ks[18], (N_DENSE, D_FF_DENSE, D_MODEL), D_FF_DENSE ** -0.5 * DEEPNORM_BETA)
    router_w = nrm(ks[19], (N_MOE, D_MODEL, N_EXPERTS), D_MODEL ** -0.5)
    moe_w_gate = nrm(ks[20], (N_MOE, N_EXPERTS, D_MODEL, D_FF_EXPERT), D_MODEL ** -0.5)
    moe_w_up = nrm(ks[21], (N_MOE, N_EXPERTS, D_MODEL, D_FF_EXPERT), D_MODEL ** -0.5)
    moe_w_down = nrm(ks[22], (N_MOE, N_EXPERTS, D_FF_EXPERT, D_MODEL), D_FF_EXPERT ** -0.5 * DEEPNORM_BETA)
    return {"x": x, "positions": positions, "w_in": w_in, "conv_w": conv_w, "a_log": a_log,
            "dt_bias": dt_bias, "gdn_norm_w": gdn_norm_w, "lam_q1": lam_q1, "lam_k1": lam_k1,
            "lam_q2": lam_q2, "lam_k2": lam_k2, "subln_w": subln_w, "w_out": w_out,
            "ln1_g": ln1_g, "ln1_b": ln1_b, "ln2_g": ln2_g, "ln2_b": ln2_b,
            "ffn_w_gate": ffn_w_gate, "ffn_w_up": ffn_w_up, "ffn_w_down": ffn_w_down,
            "router_w": router_w, "moe_w_gate": moe_w_gate, "moe_w_up": moe_w_up, "moe_w_down": moe_w_down}


def reference(x, positions, w_in, conv_w, a_log, dt_bias, gdn_norm_w, lam_q1, lam_k1, lam_q2, lam_k2,
              subln_w, w_out, ln1_g, ln1_b, ln2_g, ln2_b, ffn_w_gate, ffn_w_up, ffn_w_down,
              router_w, moe_w_gate, moe_w_up, moe_w_down):
    inv_freq = ROPE_THETA ** (-jnp.arange(0, DA_HEAD_DIM, 2, dtype=jnp.float32) / DA_HEAD_DIM)
    ang = positions.astype(jnp.float32)[..., None] * inv_freq
    ang = jnp.concatenate([ang, ang], axis=-1)
    cos = jnp.cos(ang)[:, :, None, None, :].astype(x.dtype)
    sin = jnp.sin(ang)[:, :, None, None, :].astype(x.dtype)
    for l in range(DEPTH):
        lam_init = 0.8 - 0.6 * math.exp(-0.3 * l)
        h = hybrid_mixer(x, w_in[l], conv_w[l], a_log[l], dt_bias[l], gdn_norm_w[l], lam_q1[l], lam_k1[l],
                         lam_q2[l], lam_k2[l], subln_w[l], w_out[l], cos, sin, lam_init)
        x = layer_norm(DEEPNORM_ALPHA * x + h, ln1_g[l], ln1_b[l])
        if l % 2 == 0:
            f = swiglu(x, ffn_w_gate[l // 2], ffn_w_up[l // 2], ffn_w_down[l // 2])
        else:
            f = moe_ffn(x, router_w[l // 2], moe_w_gate[l // 2], moe_w_up[l // 2], moe_w_down[l // 2])
        x = layer_norm(DEEPNORM_ALPHA * x + f, ln2_g[l], ln2_b[l])
    return x
```

```python
import functools
import math

import jax
import jax.numpy as jnp
from jax import lax
from jax.experimental import pallas as pl
from jax.experimental.pallas import tpu as pltpu

D_MODEL = 1024
DEPTH = 4
N_HEADS = 4
HEAD_W = 128
DA_HEAD_DIM = 64
GROUP_W = N_HEADS * HEAD_W
GDN_QKV_W = 3 * GROUP_W
CONV_WIDTH = 4
GDN_CHUNK = 128
N_EXPERTS = 8
TOP_K = 2
D_FF_DENSE = 2816
D_FF_EXPERT = 3584
ROPE_THETA = 10000.0
DEEPNORM_ALPHA = (2.0 * DEPTH) ** 0.25
LN_EPS = 1e-5
RMS_EPS = 1e-6
NEG_BIG = -1e30

ROW_TILE = 512
ATTN_TILE = 256
DENSE_FF_TILE = 1408
EXPERT_FF_TILE = 896
EXPERT_ROWS = 512
VMEM_LIMIT = 56 * 1024 * 1024

_BF16 = jnp.bfloat16
_F32 = jnp.float32


def _cparams(sem):
    return pltpu.CompilerParams(dimension_semantics=sem, vmem_limit_bytes=VMEM_LIMIT)


def _dot(a, b):
    return jnp.dot(a, b, preferred_element_type=_F32)


def _dot_nt(a, b):
    return lax.dot_general(a, b, (((1,), (1,)), ((), ())), preferred_element_type=_F32)


def _dot_hi(a, b):
    return jnp.dot(a, b, preferred_element_type=_F32, precision=lax.Precision.HIGHEST)


def _sigmoid(x):
    return 1.0 / (1.0 + jnp.exp(-x))


def _silu(x):
    return x * _sigmoid(x)


def _layer_norm(y, g, b):
    mu = jnp.mean(y, axis=-1, keepdims=True)
    yc = y - mu
    var = jnp.mean(yc * yc, axis=-1, keepdims=True)
    return yc * lax.rsqrt(var + LN_EPS) * g + b


def _proj_kernel(x_ref, wq_ref, wk_ref, wv_ref, wg_ref, wz_ref, wab_ref, cos_ref, sin_ref,
                 arow_ref, dtb_ref, q_ref, k_ref, v_ref, g_ref, z_ref, gb_ref):
    x = x_ref[...]
    tm = x.shape[0]
    cos = jnp.concatenate([cos_ref[...]] * N_HEADS, axis=1)
    sin = jnp.concatenate([sin_ref[...]] * N_HEADS, axis=1)
    lane = lax.broadcasted_iota(jnp.int32, (tm, GROUP_W), 1)
    first_half = (lane % DA_HEAD_DIM) < (DA_HEAD_DIM // 2)

    def rope(t):
        partner = jnp.where(first_half,
                            pltpu.roll(t, GROUP_W - DA_HEAD_DIM // 2, 1),
                            pltpu.roll(t, DA_HEAD_DIM // 2, 1))
        return t * cos + partner * sin

    q_ref[...] = (rope(_dot(x, wq_ref[...])) * (DA_HEAD_DIM ** -0.5)).astype(q_ref.dtype)
    k_ref[...] = rope(_dot(x, wk_ref[...])).astype(k_ref.dtype)
    v_ref[...] = _dot(x, wv_ref[...]).astype(v_ref.dtype)
    g_ref[...] = _dot(x, wg_ref[...])
    z_ref[...] = _dot(x, wz_ref[...])
    ab = _dot(x, wab_ref[...])
    t = ab + dtb_ref[...]
    softplus = jnp.maximum(t, 0.0) + jnp.log1p(jnp.exp(-jnp.abs(t)))
    lane_ab = lax.broadcasted_iota(jnp.int32, ab.shape, 1)
    gb_ref[...] = jnp.where(lane_ab < N_HEADS, -arow_ref[...] * softplus, _sigmoid(ab))


def _proj(xb, wq, wk, wv, wg, wz, wab, cos, sin, arow, dtb):
    T = xb.shape[0]
    tm = min(ROW_TILE, T)
    row = lambda i: (i, 0)
    full = lambda i: (0, 0)
    wspec = lambda w: pl.BlockSpec(w.shape, full)
    return pl.pallas_call(
        _proj_kernel,
        grid=(T // tm,),
        in_specs=[pl.BlockSpec((tm, D_MODEL), row), wspec(wq), wspec(wk), wspec(wv), wspec(wg),
                  wspec(wz), wspec(wab), pl.BlockSpec((tm, HEAD_W), row),
                  pl.BlockSpec((tm, HEAD_W), row), wspec(arow), wspec(dtb)],
        out_specs=[pl.BlockSpec((tm, GROUP_W), row), pl.BlockSpec((tm, GROUP_W), row),
                   pl.BlockSpec((tm, GROUP_W), row), pl.BlockSpec((tm, GDN_QKV_W), row),
                   pl.BlockSpec((tm, GROUP_W), row), pl.BlockSpec((tm, HEAD_W), row)],
        out_shape=[jax.ShapeDtypeStruct((T, GROUP_W), _BF16), jax.ShapeDtypeStruct((T, GROUP_W), _BF16),
                   jax.ShapeDtypeStruct((T, GROUP_W), _BF16), jax.ShapeDtypeStruct((T, GDN_QKV_W), _F32),
                   jax.ShapeDtypeStruct((T, GROUP_W), _F32), jax.ShapeDtypeStruct((T, HEAD_W), _F32)],
        compiler_params=_cparams(("arbitrary",)),
        name="in_proj",
    )(xb, wq, wk, wv, wg, wz, wab, cos, sin, arow, dtb)


def _attn_kernel(q_ref, k_ref, v_ref, lamv_ref, subw_ref, o_ref, *, tile, lam_init):
    qi = pl.program_id(2)
    q = q_ref[...]
    lane = lax.broadcasted_iota(jnp.int32, q.shape, 1)
    zero = jnp.zeros_like(q)
    qa = jnp.where(lane < DA_HEAD_DIM, q, zero)
    qb = jnp.where(lane >= DA_HEAD_DIM, q, zero)

    def update(s, v, m, l, acc):
        m_new = jnp.maximum(m, jnp.max(s, axis=-1, keepdims=True))
        alpha = jnp.exp(m - m_new)
        p = jnp.exp(s - m_new)
        l = alpha * l + jnp.sum(p, axis=-1, keepdims=True)
        acc = alpha * acc + _dot(p.astype(v.dtype), v)
        return m_new, l, acc

    def step(kb, carry, masked):
        m1, l1, a1, m2, l2, a2 = carry
        off = pl.multiple_of(kb * tile, tile)
        k = k_ref[pl.ds(off, tile), :]
        v = v_ref[pl.ds(off, tile), :]
        s1 = _dot_nt(qa, k)
        s2 = _dot_nt(qb, k)
        if masked:
            r = lax.broadcasted_iota(jnp.int32, s1.shape, 0)
            c = lax.broadcasted_iota(jnp.int32, s1.shape, 1)
            s1 = jnp.where(r >= c, s1, NEG_BIG)
            s2 = jnp.where(r >= c, s2, NEG_BIG)
        m1, l1, a1 = update(s1, v, m1, l1, a1)
        m2, l2, a2 = update(s2, v, m2, l2, a2)
        return m1, l1, a1, m2, l2, a2

    col = lambda val: jnp.full((tile, 1), val, _F32)
    acc0 = jnp.zeros((tile, HEAD_W), _F32)
    carry = (col(NEG_BIG), col(0.0), acc0, col(NEG_BIG), col(0.0), acc0)
    carry = lax.fori_loop(0, qi, lambda kb, c: step(kb, c, False), carry)
    m1, l1, a1, m2, l2, a2 = step(qi, carry, True)

    lamv = lamv_ref[...]
    e1 = jnp.exp(jnp.sum(lamv[0:1, :] * lamv[1:2, :], axis=-1, keepdims=True))
    e2 = jnp.exp(jnp.sum(lamv[2:3, :] * lamv[3:4, :], axis=-1, keepdims=True))
    lam = e1 - e2 + lam_init
    o = a1 / l1 - lam * (a2 / l2)
    o = o * lax.rsqrt(jnp.mean(o * o, axis=-1, keepdims=True) + RMS_EPS) * subw_ref[...]
    o_ref[...] = (o * (1.0 - lam_init)).astype(o_ref.dtype)


def _attention(q, k, v, lamv, subw, B, S, lam_init):
    tile = min(ATTN_TILE, S)
    nq = S // tile
    return pl.pallas_call(
        functools.partial(_attn_kernel, tile=tile, lam_init=lam_init),
        grid=(B, N_HEADS, nq),
        in_specs=[pl.BlockSpec((tile, HEAD_W), lambda b, h, i: (b * nq + i, h)),
                  pl.BlockSpec((S, HEAD_W), lambda b, h, i: (b, h)),
                  pl.BlockSpec((S, HEAD_W), lambda b, h, i: (b, h)),
                  pl.BlockSpec(lamv.shape, lambda b, h, i: (0, 0)),
                  pl.BlockSpec(subw.shape, lambda b, h, i: (0, 0))],
        out_specs=pl.BlockSpec((tile, HEAD_W), lambda b, h, i: (b * nq + i, h)),
        out_shape=jax.ShapeDtypeStruct((B * S, GROUP_W), _BF16),
        compiler_params=_cparams(("arbitrary", "arbitrary", "arbitrary")),
        name="diff_attn",
    )(q, k, v, lamv, subw)


def _gdn_kernel(xq_ref, xk_ref, xv_ref, cwq_ref, cwk_ref, cwv_ref, gb_ref, z_ref, nw_ref, o_ref,
                xpad, qs, ks, vs, *, seq):
    h = pl.program_id(1)
    C = GDN_CHUNK

    def conv_silu(x_ref, cw_ref):
        xpad[0:8, :] = jnp.zeros((8, HEAD_W), _F32)
        xpad[8:, :] = x_ref[...]
        cw = cw_ref[...]
        acc = xpad[pl.ds(8, seq), :] * cw[3:4, :]
        for j in range(CONV_WIDTH - 1):
            acc = acc + xpad[pl.ds(5 + j, seq), :] * cw[j:j + 1, :]
        return _silu(acc)

    def l2n(t):
        return t * lax.rsqrt(jnp.sum(t * t, axis=-1, keepdims=True) + RMS_EPS)

    qs[...] = l2n(conv_silu(xq_ref, cwq_ref)) * (HEAD_W ** -0.5)
    ks[...] = l2n(conv_silu(xk_ref, cwk_ref))
    vs[...] = conv_silu(xv_ref, cwv_ref)

    ri = lax.broadcasted_iota(jnp.int32, (C, C), 0)
    ci = lax.broadcasted_iota(jnp.int32, (C, C), 1)
    tril = ri >= ci
    strict = ri > ci
    tril_f = tril.astype(_F32)
    triu_f = (ri <= ci).astype(_F32)
    ones_f = jnp.ones((C, C), _F32)
    eye_f = (ri == ci).astype(_F32)
    lane = lax.broadcasted_iota(jnp.int32, (C, HEAD_W), 1)

    def chunk(c, state):
        r0 = pl.multiple_of(c * C, C)
        qc = qs[pl.ds(r0, C), :]
        kc = ks[pl.ds(r0, C), :]
        vc = vs[pl.ds(r0, C), :]
        gb = gb_ref[pl.ds(r0, C), :]
        g_col = jnp.sum(jnp.where(lane == h, gb, 0.0), axis=-1, keepdims=True)
        b_col = jnp.sum(jnp.where(lane == h + N_HEADS, gb, 0.0), axis=-1, keepdims=True)
        g_b = jnp.broadcast_to(g_col, (C, C))
        g_rows = _dot_hi(tril_f, g_b)
        g_cols = _dot_hi(ones_f, g_b * triu_f)
        decay = jnp.exp(jnp.where(tril, g_rows - g_cols, NEG_BIG))
        k_beta = kc * b_col
        lmat = jnp.where(strict, _dot_nt(k_beta.astype(_BF16), kc.astype(_BF16)) * decay, 0.0)
        inv = eye_f - lmat
        pw = lmat
        for _ in range(int(math.log2(C)) - 1):
            pw = _dot_hi(pw, pw)
            inv = inv + _dot_hi(inv, pw)
        e_g = jnp.exp(g_rows)
        rhs = jnp.concatenate([vc * b_col, k_beta * e_g], axis=1).astype(_BF16)
        sol = _dot(inv.astype(_BF16), rhs)
        u = sol[:, :HEAD_W]
        w = sol[:, HEAD_W:]
        qk = jnp.where(tril, _dot_nt(qc.astype(_BF16), kc.astype(_BF16)) * decay, 0.0)
        g_last = g_rows[C - 1:C, :]
        k_tail = kc * jnp.exp(g_last - g_rows)
        st_b = state.astype(_BF16)
        v_new = u - _dot(w.astype(_BF16), st_b)
        out = _dot((qc * e_g).astype(_BF16), st_b) + _dot(qk.astype(_BF16), v_new.astype(_BF16))
        state = state * jnp.exp(g_last) + _dot(k_tail.T.astype(_BF16), v_new.astype(_BF16))
        out = out * lax.rsqrt(jnp.mean(out * out, axis=-1, keepdims=True) + RMS_EPS) * nw_ref[...]
        o_ref[pl.ds(r0, C), :] = (out * _silu(z_ref[pl.ds(r0, C), :])).astype(o_ref.dtype)
        return state

    lax.fori_loop(0, seq // C, chunk, jnp.zeros((HEAD_W, HEAD_W), _F32))


def _gdn(g, conv_w, gb, z, nw, B, S):
    blk = lambda j: pl.BlockSpec((S, HEAD_W), lambda b, h: (b, h + j * N_HEADS))
    cw = lambda j: pl.BlockSpec((CONV_WIDTH, HEAD_W), lambda b, h: (0, h + j * N_HEADS))
    return pl.pallas_call(
        functools.partial(_gdn_kernel, seq=S),
        grid=(B, N_HEADS),
        in_specs=[blk(0), blk(1), blk(2), cw(0), cw(1), cw(2),
                  pl.BlockSpec((S, HEAD_W), lambda b, h: (b, 0)),
                  pl.BlockSpec((S, HEAD_W), lambda b, h: (b, h)),
                  pl.BlockSpec((1, HEAD_W), lambda b, h: (0, 0))],
        out_specs=pl.BlockSpec((S, HEAD_W), lambda b, h: (b, h)),
        out_shape=jax.ShapeDtypeStruct((B * S, GROUP_W), _BF16),
        scratch_shapes=[pltpu.VMEM((S + 8, HEAD_W), _F32), pltpu.VMEM((S, HEAD_W), _F32),
                        pltpu.VMEM((S, HEAD_W), _F32), pltpu.VMEM((S, HEAD_W), _F32)],
        compiler_params=_cparams(("arbitrary", "arbitrary")),
        name="gdn",
    )(g, g, g, conv_w, conv_w, conv_w, gb, z, nw)


def _outproj_kernel(da_ref, gd_ref, x_ref, wa_ref, wg_ref, g_ref, b_ref, *rest, with_router):
    if with_router:
        rw_ref, xo_ref, xb_ref, lg_ref = rest
    else:
        xo_ref, xb_ref = rest
    hmix = _dot(da_ref[...], wa_ref[...]) + _dot(gd_ref[...], wg_ref[...])
    y = _layer_norm(DEEPNORM_ALPHA * x_ref[...] + hmix, g_ref[...], b_ref[...])
    xo_ref[...] = y
    xb_ref[...] = y.astype(xb_ref.dtype)
    if with_router:
        lg_ref[...] = _dot_hi(y, rw_ref[...])


def _outproj(da, gd, x, wa, wg, g, b, rw=None):
    T = x.shape[0]
    tm = min(ROW_TILE, T)
    row = lambda i: (i, 0)
    full = lambda i: (0, 0)
    in_specs = [pl.BlockSpec((tm, GROUP_W), row), pl.BlockSpec((tm, GROUP_W), row),
                pl.BlockSpec((tm, D_MODEL), row), pl.BlockSpec(wa.shape, full), pl.BlockSpec(wg.shape, full),
                pl.BlockSpec(g.shape, full), pl.BlockSpec(b.shape, full)]
    out_specs = [pl.BlockSpec((tm, D_MODEL), row), pl.BlockSpec((tm, D_MODEL), row)]
    out_shape = [jax.ShapeDtypeStruct((T, D_MODEL), _F32), jax.ShapeDtypeStruct((T, D_MODEL), _BF16)]
    args = [da, gd, x, wa, wg, g, b]
    if rw is not None:
        in_specs.append(pl.BlockSpec(rw.shape, full))
        out_specs.append(pl.BlockSpec((tm, HEAD_W), row))
        out_shape.append(jax.ShapeDtypeStruct((T, HEAD_W), _F32))
        args.append(rw)
    return pl.pallas_call(
        functools.partial(_outproj_kernel, with_router=rw is not None),
        grid=(T // tm,), in_specs=in_specs, out_specs=out_specs, out_shape=out_shape,
        compiler_params=_cparams(("arbitrary",)),
        name="out_proj_ln",
    )(*args)


def _ffn_kernel(xb_ref, x_ref, wg_ref, wu_ref, wd_ref, g_ref, b_ref, xo_ref, xbo_ref, acc_ref):
    f = pl.program_id(1)
    xb = xb_ref[...]
    hidden = (_silu(_dot(xb, wg_ref[...])) * _dot(xb, wu_ref[...])).astype(_BF16)
    part = _dot(hidden, wd_ref[...])

    @pl.when(f == 0)
    def _():
        acc_ref[...] = part

    @pl.when(f > 0)
    def _():
        acc_ref[...] += part

    @pl.when(f == pl.num_programs(1) - 1)
    def _():
        y = _layer_norm(DEEPNORM_ALPHA * x_ref[...] + acc_ref[...], g_ref[...], b_ref[...])
        xo_ref[...] = y
        xbo_ref[...] = y.astype(xbo_ref.dtype)


def _ffn(xb, x, wg, wu, wd, g, b):
    T = x.shape[0]
    tm = min(ROW_TILE, T)
    tf = DENSE_FF_TILE
    row = lambda i, f: (i, 0)
    full = lambda i, f: (0, 0)
    return pl.pallas_call(
        _ffn_kernel,
        grid=(T // tm, D_FF_DENSE // tf),
        in_specs=[pl.BlockSpec((tm, D_MODEL), row), pl.BlockSpec((tm, D_MODEL), row),
                  pl.BlockSpec((D_MODEL, tf), lambda i, f: (0, f)),
                  pl.BlockSpec((D_MODEL, tf), lambda i, f: (0, f)),
                  pl.BlockSpec((tf, D_MODEL), lambda i, f: (f, 0)),
                  pl.BlockSpec(g.shape, full), pl.BlockSpec(b.shape, full)],
        out_specs=[pl.BlockSpec((tm, D_MODEL), row), pl.BlockSpec((tm, D_MODEL), row)],
        out_shape=[jax.ShapeDtypeStruct((T, D_MODEL), _F32), jax.ShapeDtypeStruct((T, D_MODEL), _BF16)],
        scratch_shapes=[pltpu.VMEM((tm, D_MODEL), _F32)],
        compiler_params=_cparams(("arbitrary", "arbitrary")),
        name="dense_ffn_ln",
    )(xb, x, wg, wu, wd, g, b)


def _expert_kernel(blk_e_ref, xg_ref, wt_ref, wg_ref, wu_ref, wd_ref, y_ref, acc_ref):
    f = pl.program_id(1)
    xb = xg_ref[...]
    hidden = (_silu(_dot(xb, wg_ref[0])) * _dot(xb, wu_ref[0])).astype(_BF16)
    part = _dot(hidden, wd_ref[0])

    @pl.when(f == 0)
    def _():
        acc_ref[...] = part

    @pl.when(f > 0)
    def _():
        acc_ref[...] += part

    @pl.when(f == pl.num_programs(1) - 1)
    def _():
        y_ref[...] = acc_ref[...] * wt_ref[...]


def _experts(blk_e, xg, wt, wg, wu, wd):
    P = xg.shape[0]
    tm = EXPERT_ROWS
    tf = EXPERT_FF_TILE
    grid_spec = pltpu.PrefetchScalarGridSpec(
        num_scalar_prefetch=1,
        grid=(P // tm, D_FF_EXPERT // tf),
        in_specs=[pl.BlockSpec((tm, D_MODEL), lambda i, f, be: (i, 0)),
                  pl.BlockSpec((tm, 1), lambda i, f, be: (i, 0)),
                  pl.BlockSpec((1, D_MODEL, tf), lambda i, f, be: (be[i], 0, f)),
                  pl.BlockSpec((1, D_MODEL, tf), lambda i, f, be: (be[i], 0, f)),
                  pl.BlockSpec((1, tf, D_MODEL), lambda i, f, be: (be[i], f, 0))],
        out_specs=pl.BlockSpec((tm, D_MODEL), lambda i, f, be: (i, 0)),
        scratch_shapes=[pltpu.VMEM((tm, D_MODEL), _F32)],
    )
    return pl.pallas_call(
        _expert_kernel, grid_spec=grid_spec,
        out_shape=jax.ShapeDtypeStruct((P, D_MODEL), _F32),
        compiler_params=_cparams(("arbitrary", "arbitrary")),
        name="expert_ffn",
    )(blk_e, xg, wt, wg, wu, wd)


def _combine_kernel(y0_ref, y1_ref, x_ref, g_ref, b_ref, xo_ref, xbo_ref):
    y = _layer_norm(DEEPNORM_ALPHA * x_ref[...] + (y0_ref[...] + y1_ref[...]), g_ref[...], b_ref[...])
    xo_ref[...] = y
    xbo_ref[...] = y.astype(xbo_ref.dtype)


def _combine(y0, y1, x, g, b):
    T = x.shape[0]
    tm = min(ROW_TILE, T)
    row = lambda i: (i, 0)
    full = lambda i: (0, 0)
    return pl.pallas_call(
        _combine_kernel,
        grid=(T // tm,),
        in_specs=[pl.BlockSpec((tm, D_MODEL), row)] * 3 + [pl.BlockSpec(g.shape, full), pl.BlockSpec(b.shape, full)],
        out_specs=[pl.BlockSpec((tm, D_MODEL), row), pl.BlockSpec((tm, D_MODEL), row)],
        out_shape=[jax.ShapeDtypeStruct((T, D_MODEL), _F32), jax.ShapeDtypeStruct((T, D_MODEL), _BF16)],
        compiler_params=_cparams(("arbitrary",)),
        name="moe_combine_ln",
    )(y0, y1, x, g, b)


def _route(logits, T):
    top_val, top_idx = lax.top_k(logits, TOP_K)
    gates = jax.nn.softmax(top_val, axis=-1)
    A = T * TOP_K
    e = top_idx.reshape(A).astype(jnp.int32)
    onehot = (e[:, None] == jnp.arange(N_EXPERTS, dtype=jnp.int32)[None, :]).astype(jnp.int32)
    rank = jnp.take_along_axis(jnp.cumsum(onehot, axis=0) - onehot, e[:, None], axis=1)[:, 0]
    counts = jnp.sum(onehot, axis=0)
    padded = ((counts + EXPERT_ROWS - 1) // EXPERT_ROWS) * EXPERT_ROWS
    pend = jnp.cumsum(padded)
    pstarts = pend - padded
    dest = pstarts[e] + rank
    P = A + N_EXPERTS * EXPERT_ROWS
    n_blk = P // EXPERT_ROWS
    tok = jnp.repeat(jnp.arange(T, dtype=jnp.int32), TOP_K)
    tok_buf = jnp.zeros((P,), jnp.int32).at[dest].set(tok)
    w_buf = jnp.zeros((P,), _F32).at[dest].set(gates.reshape(A))
    blk_e = jnp.minimum(jnp.searchsorted(pend, jnp.arange(n_blk, dtype=jnp.int32) * EXPERT_ROWS, side='right'),
                        N_EXPERTS - 1).astype(jnp.int32)
    return tok_buf, w_buf, blk_e, dest.reshape(T, TOP_K)


def _moe(xb, x, logits, wg, wu, wd, g, b):
    T = x.shape[0]
    tok_buf, w_buf, blk_e, dest = _route(logits[:, :N_EXPERTS], T)
    xg = jnp.take(xb, tok_buf, axis=0)
    y = _experts(blk_e, xg, w_buf[:, None], wg, wu, wd)
    y0 = jnp.take(y, dest[:, 0], axis=0)
    y1 = jnp.take(y, dest[:, 1], axis=0)
    return _combine(y0, y1, x, g, b)


def kernel(x, positions, w_in, conv_w, a_log, dt_bias, gdn_norm_w, lam_q1, lam_k1, lam_q2, lam_k2, subln_w, w_out, ln1_g, ln1_b, ln2_g, ln2_b, ffn_w_gate, ffn_w_up, ffn_w_down, router_w, moe_w_gate, moe_w_up, moe_w_down):
    B, S, _ = x.shape
    T = B * S
    half = DA_HEAD_DIM // 2
    inv_freq = ROPE_THETA ** (-jnp.arange(0, DA_HEAD_DIM, 2, dtype=_F32) / DA_HEAD_DIM)
    ang = positions.astype(_F32).reshape(T, 1) * inv_freq[None, :]
    cos = jnp.tile(jnp.cos(ang), (1, HEAD_W // half))
    sin = jnp.tile(jnp.concatenate([-jnp.sin(ang), jnp.sin(ang)], axis=1), (1, HEAD_W // DA_HEAD_DIM))

    c0, c1, c2, c3, c4, c5 = (GROUP_W, 2 * GROUP_W, 3 * GROUP_W, 3 * GROUP_W + GDN_QKV_W,
                              4 * GROUP_W + GDN_QKV_W, 4 * GROUP_W + GDN_QKV_W + 2 * N_HEADS)
    xf = x.reshape(T, D_MODEL)
    xb = xf.astype(_BF16)
    row = lambda v: v.reshape(1, -1).astype(_F32)
    pad_lanes = lambda v: jnp.pad(v.astype(_F32), (0, HEAD_W - v.shape[0])).reshape(1, HEAD_W)

    for l in range(DEPTH):
        lam_init = 0.8 - 0.6 * math.exp(-0.3 * l)
        w = w_in[l].astype(_BF16)
        wab = jnp.pad(w[:, c4:c5], ((0, 0), (0, HEAD_W - 2 * N_HEADS)))
        q, k, v, g, z, gb = _proj(xb, w[:, :c0], w[:, c0:c1], w[:, c1:c2], w[:, c2:c3], w[:, c3:c4], wab,
                                  cos, sin, pad_lanes(jnp.exp(a_log[l])), pad_lanes(dt_bias[l]))
        lamv = jnp.stack([lam_q1[l], lam_k1[l], lam_q2[l], lam_k2[l]]).astype(_F32)
        da = _attention(q, k, v, lamv, row(subln_w[l]), B, S, lam_init)
        gd = _gdn(g, conv_w[l].astype(_F32), gb, z, row(gdn_norm_w[l]), B, S)
        wo = w_out[l].astype(_BF16)
        if l % 2 == 0:
            xf, xb = _outproj(da, gd, xf, wo[:GROUP_W], wo[GROUP_W:], row(ln1_g[l]), row(ln1_b[l]))
            i = l // 2
            xf, xb = _ffn(xb, xf, ffn_w_gate[i].astype(_BF16), ffn_w_up[i].astype(_BF16),
                          ffn_w_down[i].astype(_BF16), row(ln2_g[l]), row(ln2_b[l]))
        else:
            i = l // 2
            rw = jnp.pad(router_w[i].astype(_F32), ((0, 0), (0, HEAD_W - N_EXPERTS)))
            xf, xb, logits = _outproj(da, gd, xf, wo[:GROUP_W], wo[GROUP_W:], row(ln1_g[l]), row(ln1_b[l]), rw)
            xf, xb = _moe(xb, xf, logits, moe_w_gate[i].astype(_BF16), moe_w_up[i].astype(_BF16),
                          moe_w_down[i].astype(_BF16), row(ln2_g[l]), row(ln2_b[l]))
    return xf.reshape(B, S, D_MODEL)
```

```python
import functools
import math

import jax
import jax.numpy as jnp
from jax import lax
from jax.experimental import pallas as pl
from jax.experimental.pallas import tpu as pltpu

D_MODEL = 1024
DEPTH = 4
N_HEADS = 4
HEAD_W = 128
DA_HEAD_DIM = 64
GROUP_W = N_HEADS * HEAD_W
GDN_QKV_W = 3 * GROUP_W
CONV_WIDTH = 4
GDN_CHUNK = 128
GDN_INTRA_UNROLL = 8
N_EXPERTS = 8
TOP_K = 2
D_FF_DENSE = 2816
D_FF_EXPERT = 3584
ROPE_THETA = 10000.0
DEEPNORM_ALPHA = (2.0 * DEPTH) ** 0.25
LN_EPS = 1e-5
RMS_EPS = 1e-6
NEG_BIG = -1e30
LOG2_E = math.log2(math.e)

ROW_TILE = 512
ATTN_TILE = 256
DENSE_FF_TILE = 1408
EXPERT_FF_TILE = 896
EXPERT_ROWS = 512
MOE_ROW_TILE = 256
VMEM_LIMIT = 56 * 1024 * 1024

_BF16 = jnp.bfloat16
_F32 = jnp.float32


def _cparams(sem):
    return pltpu.CompilerParams(dimension_semantics=sem, vmem_limit_bytes=VMEM_LIMIT)


def _dot(a, b):
    return jnp.dot(a, b, preferred_element_type=_F32)


def _dot_nt(a, b):
    return lax.dot_general(a, b, (((1,), (1,)), ((), ())), preferred_element_type=_F32)


def _dot_hi(a, b):
    return jnp.dot(a, b, preferred_element_type=_F32, precision=lax.Precision.HIGHEST)


def _sigmoid(x):
    return 1.0 / (1.0 + jnp.exp(-x))


def _silu(x):
    return x * _sigmoid(x)


def _layer_norm(y, g, b):
    mu = jnp.mean(y, axis=-1, keepdims=True)
    yc = y - mu
    var = jnp.mean(yc * yc, axis=-1, keepdims=True)
    return yc * lax.rsqrt(var + LN_EPS) * g + b


def _proj_kernel(x_ref, wq_ref, wk_ref, wvt_ref, wg_ref, wz_ref, wab_ref, cos_ref, sin_ref,
                 arow_ref, dtb_ref, q_ref, k_ref, vt_ref, g_ref, z_ref, gb_ref):
    x = x_ref[...]
    tm = x.shape[0]
    cos = jnp.concatenate([cos_ref[...]] * N_HEADS, axis=1)
    sin = jnp.concatenate([sin_ref[...]] * N_HEADS, axis=1)
    lane = lax.broadcasted_iota(jnp.int32, (tm, GROUP_W), 1)
    first_half = (lane % DA_HEAD_DIM) < (DA_HEAD_DIM // 2)

    def rope(t):
        partner = jnp.where(first_half,
                            pltpu.roll(t, GROUP_W - DA_HEAD_DIM // 2, 1),
                            pltpu.roll(t, DA_HEAD_DIM // 2, 1))
        return t * cos + partner * sin

    q_ref[...] = (rope(_dot(x, wq_ref[...])) * (DA_HEAD_DIM ** -0.5 * LOG2_E)).astype(q_ref.dtype)
    k_ref[...] = rope(_dot(x, wk_ref[...])).astype(k_ref.dtype)
    kv_tile = vt_ref.shape[2]
    for t in range(vt_ref.shape[0]):
        vt_ref[t] = _dot_nt(wvt_ref[...], x[t * kv_tile:(t + 1) * kv_tile, :]).astype(vt_ref.dtype)
    g_ref[...] = _dot(x, wg_ref[...])
    z_ref[...] = _dot(x, wz_ref[...])
    ab = _dot(x, wab_ref[...])
    t = ab + dtb_ref[...]
    softplus = jnp.maximum(t, 0.0) + jnp.log1p(jnp.exp(-jnp.abs(t)))
    lane_ab = lax.broadcasted_iota(jnp.int32, ab.shape, 1)
    gb_ref[...] = jnp.where(lane_ab < N_HEADS, -arow_ref[...] * softplus, _sigmoid(ab))


def _proj(xb, wq, wk, wvt, wg, wz, wab, cos, sin, arow, dtb, kv_tile):
    T = xb.shape[0]
    tm = min(ROW_TILE, T)
    row = lambda i: (i, 0)
    full = lambda i: (0, 0)
    wspec = lambda w: pl.BlockSpec(w.shape, full)
    return pl.pallas_call(
        _proj_kernel,
        grid=(T // tm,),
        in_specs=[pl.BlockSpec((tm, D_MODEL), row), wspec(wq), wspec(wk), wspec(wvt), wspec(wg),
                  wspec(wz), wspec(wab), pl.BlockSpec((tm, HEAD_W), row),
                  pl.BlockSpec((tm, HEAD_W), row), wspec(arow), wspec(dtb)],
        out_specs=[pl.BlockSpec((tm, GROUP_W), row), pl.BlockSpec((tm, GROUP_W), row),
                   pl.BlockSpec((tm // kv_tile, GROUP_W, kv_tile), lambda i: (i, 0, 0)),
                   pl.BlockSpec((tm, GDN_QKV_W), row),
                   pl.BlockSpec((tm, GROUP_W), row), pl.BlockSpec((tm, HEAD_W), row)],
        out_shape=[jax.ShapeDtypeStruct((T, GROUP_W), _BF16), jax.ShapeDtypeStruct((T, GROUP_W), _BF16),
                   jax.ShapeDtypeStruct((T // kv_tile, GROUP_W, kv_tile), _BF16),
                   jax.ShapeDtypeStruct((T, GDN_QKV_W), _F32),
                   jax.ShapeDtypeStruct((T, GROUP_W), _F32), jax.ShapeDtypeStruct((T, HEAD_W), _F32)],
        compiler_params=_cparams(("arbitrary",)),
        name="in_proj",
    )(xb, wq, wk, wvt, wg, wz, wab, cos, sin, arow, dtb)


def _attn_kernel(q_ref, k_ref, vt_ref, lamv_ref, subw_ref, o_ref, *, tile, lam_init):
    qi = pl.program_id(1)
    lane = lax.broadcasted_iota(jnp.int32, (tile, HEAD_W), 1)
    qm = []
    for h in range(N_HEADS):
        q = q_ref[:, h * HEAD_W:(h + 1) * HEAD_W]
        zero = jnp.zeros_like(q)
        qm.append(jnp.where(lane < DA_HEAD_DIM, q, zero))
        qm.append(jnp.where(lane >= DA_HEAD_DIM, q, zero))

    def step(kb, carry, masked):
        off = pl.multiple_of(kb * tile, tile)
        J = range(2 * N_HEADS)
        ks = [k_ref[pl.ds(off, tile), h * HEAD_W:(h + 1) * HEAD_W] for h in range(N_HEADS)]
        vts = [vt_ref[kb, h * HEAD_W:(h + 1) * HEAD_W, :] for h in range(N_HEADS)]
        s = [_dot_nt(ks[j // 2], qm[j]) for j in J]
        if masked:
            key = lax.broadcasted_iota(jnp.int32, (tile, tile), 0)
            qry = lax.broadcasted_iota(jnp.int32, (tile, tile), 1)
            s = [jnp.where(qry >= key, t, NEG_BIG) for t in s]
        m_new = [jnp.maximum(carry[j][0], jnp.max(s[j], axis=0, keepdims=True)) for j in J]
        alpha = [jnp.exp2(carry[j][0] - m_new[j]) for j in J]
        p = [jnp.exp2(s[j] - m_new[j]) for j in J]
        l = [alpha[j] * carry[j][1] + jnp.sum(p[j], axis=0, keepdims=True) for j in J]
        pv = [_dot(vts[j // 2], p[j].astype(_BF16)) for j in J]
        return tuple((m_new[j], l[j], alpha[j] * carry[j][2] + pv[j]) for j in J)

    rowv = lambda val: jnp.full((1, tile), val, _F32)
    init = (rowv(NEG_BIG), rowv(0.0), jnp.zeros((HEAD_W, tile), _F32))
    carry = lax.fori_loop(0, qi, lambda kb, c: step(kb, c, False), (init,) * (2 * N_HEADS))
    carry = step(qi, carry, True)

    lamv = lamv_ref[...]
    e1 = jnp.exp(jnp.sum(lamv[0:1, :] * lamv[1:2, :], axis=-1, keepdims=True))
    e2 = jnp.exp(jnp.sum(lamv[2:3, :] * lamv[3:4, :], axis=-1, keepdims=True))
    lam = e1 - e2 + lam_init
    for h in range(N_HEADS):
        (_, l1, a1), (_, l2, a2) = carry[2 * h], carry[2 * h + 1]
        o = a1 / l1 - lam * (a2 / l2)
        o = o * lax.rsqrt(jnp.mean(o * o, axis=0, keepdims=True) + RMS_EPS)
        o_ref[:, h * HEAD_W:(h + 1) * HEAD_W] = (o.T * subw_ref[...] * (1.0 - lam_init)).astype(o_ref.dtype)


def _attention(q, k, vt, lamv, subw, B, S, lam_init):
    tile = min(ATTN_TILE, S)
    nq = S // tile
    return pl.pallas_call(
        functools.partial(_attn_kernel, tile=tile, lam_init=lam_init),
        grid=(B, nq),
        in_specs=[pl.BlockSpec((tile, GROUP_W), lambda b, i: (b * nq + i, 0)),
                  pl.BlockSpec((S, GROUP_W), lambda b, i: (b, 0)),
                  pl.BlockSpec((nq, GROUP_W, tile), lambda b, i: (b, 0, 0)),
                  pl.BlockSpec(lamv.shape, lambda b, i: (0, 0)),
                  pl.BlockSpec(subw.shape, lambda b, i: (0, 0))],
        out_specs=pl.BlockSpec((tile, GROUP_W), lambda b, i: (b * nq + i, 0)),
        out_shape=jax.ShapeDtypeStruct((B * S, GROUP_W), _BF16),
        compiler_params=_cparams(("arbitrary", "arbitrary")),
        name="diff_attn",
    )(q, k, vt, lamv, subw)


def _gdn_kernel(xq_ref, xk_ref, xv_ref, cwq_ref, cwk_ref, cwv_ref, gb_ref, z_ref, nw_ref, o_ref,
                xpad, qs, ks, vs, lhs_s, ob_s, cd_s, out_s, *, seq):
    h = pl.program_id(1)
    C = GDN_CHUNK

    def conv_silu(x_ref, cw_ref):
        xpad[0:8, :] = jnp.zeros((8, HEAD_W), _F32)
        xpad[8:, :] = x_ref[...]
        cw = cw_ref[...]
        acc = xpad[pl.ds(8, seq), :] * cw[3:4, :]
        for j in range(CONV_WIDTH - 1):
            acc = acc + xpad[pl.ds(5 + j, seq), :] * cw[j:j + 1, :]
        return _silu(acc)

    def l2n(t):
        return t * lax.rsqrt(jnp.sum(t * t, axis=-1, keepdims=True) + RMS_EPS)

    qs[...] = l2n(conv_silu(xq_ref, cwq_ref)) * (HEAD_W ** -0.5)
    ks[...] = l2n(conv_silu(xk_ref, cwk_ref))
    vs[...] = conv_silu(xv_ref, cwv_ref)

    ri = lax.broadcasted_iota(jnp.int32, (C, C), 0)
    ci = lax.broadcasted_iota(jnp.int32, (C, C), 1)
    tril = ri >= ci
    strict = ri > ci
    tril_b = tril.astype(_BF16)
    eye_f = (ri == ci).astype(_F32)
    lane = lax.broadcasted_iota(jnp.int32, (C, HEAD_W), 1)
    n_chunks = seq // C
    unroll = math.gcd(n_chunks, GDN_INTRA_UNROLL)

    def intra_group(i, carry):
        cs = [i * unroll + j for j in range(unroll)]
        r0 = [pl.multiple_of(c * C, C) for c in cs]
        r2 = [pl.multiple_of(c * 2 * C, 2 * C) for c in cs]
        U = range(unroll)
        kc = [ks[pl.ds(r, C), :] for r in r0]
        gbs = [gb_ref[pl.ds(r, C), :] for r in r0]
        g_col = [jnp.sum(jnp.where(lane == h, gb, 0.0), axis=-1, keepdims=True) for gb in gbs]
        b_col = [jnp.sum(jnp.where(lane == h + N_HEADS, gb, 0.0), axis=-1, keepdims=True) for gb in gbs]
        g_b = [jnp.broadcast_to(g, (C, C)) for g in g_col]
        g_hi = [g.astype(_BF16) for g in g_b]
        g_r1 = [g_b[j] - g_hi[j].astype(_F32) for j in U]
        g_mid = [g.astype(_BF16) for g in g_r1]
        g_lo = [(g_r1[j] - g_mid[j].astype(_F32)).astype(_BF16) for j in U]
        g_rows = [_dot(tril_b, g_hi[j]) + _dot(tril_b, g_mid[j]) + _dot(tril_b, g_lo[j]) for j in U]
        decay = [jnp.exp(jnp.where(tril, g - g.T, NEG_BIG)) for g in g_rows]
        k_beta = [kc[j] * b_col[j] for j in U]
        kc_b = [k.astype(_BF16) for k in kc]
        kk = [_dot_nt(k_beta[j].astype(_BF16), kc_b[j]) for j in U]
        lmat = [jnp.where(strict, kk[j] * decay[j], 0.0) for j in U]
        inv = [eye_f - l for l in lmat]
        pw = lmat
        for _ in range(int(math.log2(C)) - 1):
            pw_b = [p.astype(_BF16) for p in pw]
            pw = [_dot(p, p) for p in pw_b]
            upd = [_dot(inv[j].astype(_BF16), pw[j].astype(_BF16)) for j in U]
            inv = [inv[j] + upd[j] for j in U]
        l_hi = [l.astype(_BF16) for l in lmat]
        l_lo = [(lmat[j] - l_hi[j].astype(_F32)).astype(_BF16) for j in U]
        x_hi = [x.astype(_BF16) for x in inv]
        x_lo = [(inv[j] - x_hi[j].astype(_F32)).astype(_BF16) for j in U]
        lx = [_dot(l_hi[j], x_hi[j]) + _dot(l_hi[j], x_lo[j]) + _dot(l_lo[j], x_hi[j]) for j in U]
        resid = [eye_f - inv[j] - lx[j] for j in U]
        inv = [inv[j] + _dot(x_hi[j], resid[j].astype(_BF16)) for j in U]
        e_g = [jnp.exp(g) for g in g_rows]
        rhs = [jnp.concatenate([vs[pl.ds(r0[j], C), :] * b_col[j], k_beta[j] * e_g[j]], axis=1).astype(_BF16)
               for j in U]
        sol = [_dot(inv[j].astype(_BF16), rhs[j]).astype(_BF16) for j in U]
        qc = [qs[pl.ds(r, C), :] for r in r0]
        qkm = [_dot_nt(qc[j].astype(_BF16), kc_b[j]) for j in U]
        g_last = [g[C - 1:C, :] for g in g_rows]
        k_tail = [kc[j] * jnp.exp(g_last[j] - g_rows[j]) for j in U]
        lhs = [jnp.concatenate([jnp.where(tril, qkm[j] * decay[j], 0.0), k_tail[j].T], axis=0).astype(_BF16)
               for j in U]
        prod = [_dot(lhs[j], sol[j]) for j in U]
        for j in U:
            ob_s[pl.ds(r2[j], 2 * C), :] = prod[j][:, :HEAD_W]
            lhs_s[pl.ds(r2[j], C), :] = (qc[j] * e_g[j] - prod[j][:C, HEAD_W:]).astype(_BF16)
            lhs_s[pl.ds(r2[j] + C, C), :] = (-prod[j][C:, HEAD_W:]).astype(_BF16)
            cd_s[pl.ds(pl.multiple_of(cs[j] * 8, 8), 8), :] = jnp.broadcast_to(jnp.exp(g_last[j]), (8, HEAD_W))
        return carry

    lax.fori_loop(0, n_chunks // unroll, intra_group, 0)

    def recur(c, state):
        r0 = pl.multiple_of(c * C, C)
        r2 = pl.multiple_of(c * 2 * C, 2 * C)
        res = _dot(lhs_s[pl.ds(r2, 2 * C), :], state.astype(_BF16)) + ob_s[pl.ds(r2, 2 * C), :]
        out_s[pl.ds(r0, C), :] = res[:C, :]
        return state * cd_s[pl.ds(pl.multiple_of(c * 8, 8), 1), :] + res[C:, :]

    lax.fori_loop(0, n_chunks, recur, jnp.zeros((HEAD_W, HEAD_W), _F32))

    out = out_s[...]
    out = out * lax.rsqrt(jnp.mean(out * out, axis=-1, keepdims=True) + RMS_EPS) * nw_ref[...]
    o_ref[...] = (out * _silu(z_ref[...])).astype(o_ref.dtype)


def _gdn(g, conv_w, gb, z, nw, B, S):
    blk = lambda j: pl.BlockSpec((S, HEAD_W), lambda b, h: (b, h + j * N_HEADS))
    cw = lambda j: pl.BlockSpec((CONV_WIDTH, HEAD_W), lambda b, h: (0, h + j * N_HEADS))
    return pl.pallas_call(
        functools.partial(_gdn_kernel, seq=S),
        grid=(B, N_HEADS),
        in_specs=[blk(0), blk(1), blk(2), cw(0), cw(1), cw(2),
                  pl.BlockSpec((S, HEAD_W), lambda b, h: (b, 0)),
                  pl.BlockSpec((S, HEAD_W), lambda b, h: (b, h)),
                  pl.BlockSpec((1, HEAD_W), lambda b, h: (0, 0))],
        out_specs=pl.BlockSpec((S, HEAD_W), lambda b, h: (b, h)),
        out_shape=jax.ShapeDtypeStruct((B * S, GROUP_W), _BF16),
        scratch_shapes=[pltpu.VMEM((S + 8, HEAD_W), _F32), pltpu.VMEM((S, HEAD_W), _F32),
                        pltpu.VMEM((S, HEAD_W), _F32), pltpu.VMEM((S, HEAD_W), _F32),
                        pltpu.VMEM((2 * S, HEAD_W), _BF16),
                        pltpu.VMEM((2 * S, HEAD_W), _F32),
                        pltpu.VMEM((S // GDN_CHUNK * 8, HEAD_W), _F32),
                        pltpu.VMEM((S, HEAD_W), _F32)],
        compiler_params=_cparams(("arbitrary", "arbitrary")),
        name="gdn",
    )(g, g, g, conv_w, conv_w, conv_w, gb, z, nw)


def _outproj_kernel(da_ref, gd_ref, x_ref, wa_ref, wg_ref, g_ref, b_ref, *rest, with_router):
    if with_router:
        rw_ref, xo_ref, xb_ref, lg_ref = rest
    else:
        xo_ref, xb_ref = rest
    hmix = _dot(da_ref[...], wa_ref[...]) + _dot(gd_ref[...], wg_ref[...])
    y = _layer_norm(DEEPNORM_ALPHA * x_ref[...] + hmix, g_ref[...], b_ref[...])
    xo_ref[...] = y
    xb_ref[...] = y.astype(xb_ref.dtype)
    if with_router:
        lg_ref[...] = _dot_hi(y, rw_ref[...])


def _outproj(da, gd, x, wa, wg, g, b, rw=None):
    T = x.shape[0]
    tm = min(ROW_TILE, T)
    row = lambda i: (i, 0)
    full = lambda i: (0, 0)
    in_specs = [pl.BlockSpec((tm, GROUP_W), row), pl.BlockSpec((tm, GROUP_W), row),
                pl.BlockSpec((tm, D_MODEL), row), pl.BlockSpec(wa.shape, full), pl.BlockSpec(wg.shape, full),
                pl.BlockSpec(g.shape, full), pl.BlockSpec(b.shape, full)]
    out_specs = [pl.BlockSpec((tm, D_MODEL), row), pl.BlockSpec((tm, D_MODEL), row)]
    out_shape = [jax.ShapeDtypeStruct((T, D_MODEL), _F32), jax.ShapeDtypeStruct((T, D_MODEL), _BF16)]
    args = [da, gd, x, wa, wg, g, b]
    if rw is not None:
        in_specs.append(pl.BlockSpec(rw.shape, full))
        out_specs.append(pl.BlockSpec((tm, HEAD_W), row))
        out_shape.append(jax.ShapeDtypeStruct((T, HEAD_W), _F32))
        args.append(rw)
    return pl.pallas_call(
        functools.partial(_outproj_kernel, with_router=rw is not None),
        grid=(T // tm,), in_specs=in_specs, out_specs=out_specs, out_shape=out_shape,
        compiler_params=_cparams(("arbitrary",)),
        name="out_proj_ln",
    )(*args)


def _ffn_kernel(xb_ref, x_ref, wg_ref, wu_ref, wd_ref, g_ref, b_ref, xo_ref, xbo_ref, acc_ref):
    f = pl.program_id(1)
    xb = xb_ref[...]
    hidden = (_silu(_dot(xb, wg_ref[...])) * _dot(xb, wu_ref[...])).astype(_BF16)
    part = _dot(hidden, wd_ref[...])

    @pl.when(f == 0)
    def _():
        acc_ref[...] = part

    @pl.when(f > 0)
    def _():
        acc_ref[...] += part

    @pl.when(f == pl.num_programs(1) - 1)
    def _():
        y = _layer_norm(DEEPNORM_ALPHA * x_ref[...] + acc_ref[...], g_ref[...], b_ref[...])
        xo_ref[...] = y
        xbo_ref[...] = y.astype(xbo_ref.dtype)


def _ffn(xb, x, wg, wu, wd, g, b):
    T = x.shape[0]
    tm = min(ROW_TILE, T)
    tf = DENSE_FF_TILE
    row = lambda i, f: (i, 0)
    full = lambda i, f: (0, 0)
    return pl.pallas_call(
        _ffn_kernel,
        grid=(T // tm, D_FF_DENSE // tf),
        in_specs=[pl.BlockSpec((tm, D_MODEL), row), pl.BlockSpec((tm, D_MODEL), row),
                  pl.BlockSpec((D_MODEL, tf), lambda i, f: (0, f)),
                  pl.BlockSpec((D_MODEL, tf), lambda i, f: (0, f)),
                  pl.BlockSpec((tf, D_MODEL), lambda i, f: (f, 0)),
                  pl.BlockSpec(g.shape, full), pl.BlockSpec(b.shape, full)],
        out_specs=[pl.BlockSpec((tm, D_MODEL), row), pl.BlockSpec((tm, D_MODEL), row)],
        out_shape=[jax.ShapeDtypeStruct((T, D_MODEL), _F32), jax.ShapeDtypeStruct((T, D_MODEL), _BF16)],
        scratch_shapes=[pltpu.VMEM((tm, D_MODEL), _F32)],
        compiler_params=_cparams(("arbitrary", "arbitrary")),
        name="dense_ffn_ln",
    )(xb, x, wg, wu, wd, g, b)


def _dispatch_kernel(dest_ref, x_hbm, slots_in, slots_out, sem, *, tm):
    del slots_in
    i = pl.program_id(0)
    slot = i % 2

    def issue(r, c):
        t = i * tm + r
        for k in range(TOP_K):
            d = dest_ref[0, 0, TOP_K * r + k]
            pltpu.make_async_copy(x_hbm.at[pl.ds(t, 1), :], slots_out.at[pl.ds(d, 1), :], sem.at[slot]).start()
        return c

    lax.fori_loop(0, tm, issue, 0, unroll=8)

    def wait_step(s):
        span = slots_out.at[pl.ds(0, TOP_K * tm), :]
        pltpu.make_async_copy(span, span, sem.at[s]).wait()

    @pl.when(i > 0)
    def _():
        wait_step(1 - slot)

    @pl.when(i == pl.num_programs(0) - 1)
    def _():
        wait_step(slot)


def _dispatch(dest3, x, P):
    T = x.shape[0]
    tm = dest3.shape[2] // TOP_K
    return pl.pallas_call(
        functools.partial(_dispatch_kernel, tm=tm),
        grid=(T // tm,),
        in_specs=[pl.BlockSpec((1, 1, TOP_K * tm), lambda i: (i, 0, 0), memory_space=pltpu.SMEM),
                  pl.BlockSpec(memory_space=pl.ANY), pl.BlockSpec(memory_space=pl.ANY)],
        out_specs=pl.BlockSpec(memory_space=pl.ANY),
        out_shape=jax.ShapeDtypeStruct((P, D_MODEL), _F32),
        scratch_shapes=[pltpu.SemaphoreType.DMA((2,))],
        input_output_aliases={2: 0},
        compiler_params=_cparams(("arbitrary",)),
        name="moe_dispatch",
    )(dest3, x, jnp.zeros((P, D_MODEL), _F32))


def _expert_kernel(blk_e_ref, xg_ref, wg_ref, wu_ref, wd_ref, y_ref, xb_ref, acc_ref):
    f = pl.program_id(1)

    @pl.when(f == 0)
    def _():
        xb_ref[...] = xg_ref[...].astype(_BF16)

    xb = xb_ref[...]
    hidden = (_silu(_dot(xb, wg_ref[0])) * _dot(xb, wu_ref[0])).astype(_BF16)
    part = _dot(hidden, wd_ref[0])

    @pl.when(f == 0)
    def _():
        acc_ref[...] = part

    @pl.when(f > 0)
    def _():
        acc_ref[...] += part

    @pl.when(f == pl.num_programs(1) - 1)
    def _():
        y_ref[...] = acc_ref[...]


def _experts(blk_e, xg, wg, wu, wd):
    P = xg.shape[0]
    tm = EXPERT_ROWS
    tf = EXPERT_FF_TILE
    grid_spec = pltpu.PrefetchScalarGridSpec(
        num_scalar_prefetch=1,
        grid=(P // tm, D_FF_EXPERT // tf),
        in_specs=[pl.BlockSpec((tm, D_MODEL), lambda i, f, be: (i, 0)),
                  pl.BlockSpec((1, D_MODEL, tf), lambda i, f, be: (be[i], 0, f)),
                  pl.BlockSpec((1, D_MODEL, tf), lambda i, f, be: (be[i], 0, f)),
                  pl.BlockSpec((1, tf, D_MODEL), lambda i, f, be: (be[i], f, 0))],
        out_specs=pl.BlockSpec((tm, D_MODEL), lambda i, f, be: (i, 0)),
        scratch_shapes=[pltpu.VMEM((tm, D_MODEL), _BF16), pltpu.VMEM((tm, D_MODEL), _F32)],
    )
    return pl.pallas_call(
        _expert_kernel, grid_spec=grid_spec,
        out_shape=jax.ShapeDtypeStruct((P, D_MODEL), _F32),
        compiler_params=_cparams(("arbitrary", "arbitrary")),
        name="expert_ffn",
    )(blk_e, xg, wg, wu, wd)


def _combine_kernel(cur_ref, nxt_ref, y_hbm, gate_ref, x_ref, g_ref, b_ref, xo_ref, xbo_ref, buf, sem, *, tm):
    i = pl.program_id(0)
    slot = i % 2

    def issue(idx_ref, s):
        def body(r, c):
            for k in range(TOP_K):
                d = idx_ref[0, 0, TOP_K * r + k]
                pltpu.make_async_copy(y_hbm.at[pl.ds(d, 1), :], buf.at[s, k, pl.ds(r, 1), :], sem.at[s]).start()
            return c
        lax.fori_loop(0, tm, body, 0, unroll=8)

    @pl.when(i == 0)
    def _():
        issue(cur_ref, 0)

    @pl.when(i + 1 < pl.num_programs(0))
    def _():
        issue(nxt_ref, 1 - slot)

    pltpu.make_async_copy(buf.at[slot], buf.at[slot], sem.at[slot]).wait()
    gates = gate_ref[...]
    f = gates[:, 0:1] * buf[slot, 0] + gates[:, 1:2] * buf[slot, 1]
    y = _layer_norm(DEEPNORM_ALPHA * x_ref[...] + f, g_ref[...], b_ref[...])
    xo_ref[...] = y
    xbo_ref[...] = y.astype(xbo_ref.dtype)


def _combine(dest3, y, gates, x, g, b):
    T = x.shape[0]
    tm = dest3.shape[2] // TOP_K
    n = T // tm
    row = lambda i: (i, 0)
    full = lambda i: (0, 0)
    smem = lambda imap: pl.BlockSpec((1, 1, TOP_K * tm), imap, memory_space=pltpu.SMEM)
    return pl.pallas_call(
        functools.partial(_combine_kernel, tm=tm),
        grid=(n,),
        in_specs=[smem(lambda i: (i, 0, 0)), smem(lambda i: (jnp.minimum(i + 1, n - 1), 0, 0)),
                  pl.BlockSpec(memory_space=pl.ANY), pl.BlockSpec((tm, TOP_K), row),
                  pl.BlockSpec((tm, D_MODEL), row), pl.BlockSpec(g.shape, full), pl.BlockSpec(b.shape, full)],
        out_specs=[pl.BlockSpec((tm, D_MODEL), row), pl.BlockSpec((tm, D_MODEL), row)],
        out_shape=[jax.ShapeDtypeStruct((T, D_MODEL), _F32), jax.ShapeDtypeStruct((T, D_MODEL), _BF16)],
        scratch_shapes=[pltpu.VMEM((2, TOP_K, tm, D_MODEL), _F32), pltpu.SemaphoreType.DMA((2,))],
        compiler_params=_cparams(("arbitrary",)),
        name="moe_combine_ln",
    )(dest3, dest3, y, gates, x, g, b)


def _route(logits, T):
    eid = jnp.arange(N_EXPERTS, dtype=jnp.int32)[None, :]
    i1 = jnp.argmax(logits, axis=-1).astype(jnp.int32)
    v1 = jnp.max(logits, axis=-1)
    rest = jnp.where(eid == i1[:, None], -jnp.inf, logits)
    i2 = jnp.argmax(rest, axis=-1).astype(jnp.int32)
    v2 = jnp.max(rest, axis=-1)
    e2 = jnp.exp(v2 - v1)
    gates = jnp.stack([1.0 / (1.0 + e2), e2 / (1.0 + e2)], axis=-1)
    e = jnp.stack([i1, i2], axis=-1).reshape(T * TOP_K)
    onehot = (e[:, None] == eid).astype(jnp.int32)
    rank = jnp.sum((jnp.cumsum(onehot, axis=0) - onehot) * onehot, axis=-1)
    counts = jnp.sum(onehot, axis=0)
    padded = ((counts + EXPERT_ROWS - 1) // EXPERT_ROWS) * EXPERT_ROWS
    pend = jnp.cumsum(padded)
    dest = jnp.sum(onehot * (pend - padded)[None, :], axis=-1) + rank
    n_blk = (T * TOP_K) // EXPERT_ROWS + N_EXPERTS
    blk_start = jnp.arange(n_blk, dtype=jnp.int32) * EXPERT_ROWS
    blk_e = jnp.minimum(jnp.sum((blk_start[:, None] >= pend[None, :]).astype(jnp.int32), axis=-1), N_EXPERTS - 1)
    return gates, dest.astype(jnp.int32), blk_e.astype(jnp.int32)


def _moe(x, logits, wg, wu, wd, g, b):
    T = x.shape[0]
    gates, dest, blk_e = _route(logits[:, :N_EXPERTS], T)
    tm = min(MOE_ROW_TILE, T)
    dest3 = dest.reshape(T // tm, 1, TOP_K * tm)
    P = T * TOP_K + N_EXPERTS * EXPERT_ROWS
    xg = _dispatch(dest3, x, P)
    y = _experts(blk_e, xg, wg, wu, wd)
    return _combine(dest3, y, gates, x, g, b)


def kernel(x, positions, w_in, conv_w, a_log, dt_bias, gdn_norm_w, lam_q1, lam_k1, lam_q2, lam_k2, subln_w, w_out, ln1_g, ln1_b, ln2_g, ln2_b, ffn_w_gate, ffn_w_up, ffn_w_down, router_w, moe_w_gate, moe_w_up, moe_w_down):
    B, S, _ = x.shape
    T = B * S
    half = DA_HEAD_DIM // 2
    inv_freq = ROPE_THETA ** (-jnp.arange(0, DA_HEAD_DIM, 2, dtype=_F32) / DA_HEAD_DIM)
    ang = positions.astype(_F32).reshape(T, 1) * inv_freq[None, :]
    cos = jnp.tile(jnp.cos(ang), (1, HEAD_W // half))
    sin = jnp.tile(jnp.concatenate([-jnp.sin(ang), jnp.sin(ang)], axis=1), (1, HEAD_W // DA_HEAD_DIM))

    c0, c1, c2, c3, c4, c5 = (GROUP_W, 2 * GROUP_W, 3 * GROUP_W, 3 * GROUP_W + GDN_QKV_W,
                              4 * GROUP_W + GDN_QKV_W, 4 * GROUP_W + GDN_QKV_W + 2 * N_HEADS)
    xf = x.reshape(T, D_MODEL)
    xb = xf.astype(_BF16)
    row = lambda v: v.reshape(1, -1).astype(_F32)
    pad_lanes = lambda v: jnp.pad(v.astype(_F32), (0, HEAD_W - v.shape[0])).reshape(1, HEAD_W)

    for l in range(DEPTH):
        lam_init = 0.8 - 0.6 * math.exp(-0.3 * l)
        w = w_in[l].astype(_BF16)
        wab = jnp.pad(w[:, c4:c5], ((0, 0), (0, HEAD_W - 2 * N_HEADS)))
        q, k, vt, g, z, gb = _proj(xb, w[:, :c0], w[:, c0:c1], w[:, c1:c2].T, w[:, c2:c3], w[:, c3:c4], wab,
                                   cos, sin, pad_lanes(jnp.exp(a_log[l])), pad_lanes(dt_bias[l]),
                                   min(ATTN_TILE, S))
        lamv = jnp.stack([lam_q1[l], lam_k1[l], lam_q2[l], lam_k2[l]]).astype(_F32)
        da = _attention(q, k, vt, lamv, row(subln_w[l]), B, S, lam_init)
        gd = _gdn(g, conv_w[l].astype(_F32), gb, z, row(gdn_norm_w[l]), B, S)
        wo = w_out[l].astype(_BF16)
        if l % 2 == 0:
            xf, xb = _outproj(da, gd, xf, wo[:GROUP_W], wo[GROUP_W:], row(ln1_g[l]), row(ln1_b[l]))
            i = l // 2
            xf, xb = _ffn(xb, xf, ffn_w_gate[i].astype(_BF16), ffn_w_up[i].astype(_BF16),
                          ffn_w_down[i].astype(_BF16), row(ln2_g[l]), row(ln2_b[l]))
        else:
            i = l // 2
            rw = jnp.pad(router_w[i].astype(_F32), ((0, 0), (0, HEAD_W - N_EXPERTS)))
            xf, xb, logits = _outproj(da, gd, xf, wo[:GROUP_W], wo[GROUP_W:], row(ln1_g[l]), row(ln1_b[l]), rw)
            xf, xb = _moe(xf, logits, moe_w_gate[i].astype(_BF16), moe_w_up[i].astype(_BF16),
                          moe_w_down[i].astype(_BF16), row(ln2_g[l]), row(ln2_b[l]))
    return xf.reshape(B, S, D_MODEL)
```

```python
import functools
import math

import jax
import jax.numpy as jnp
from jax import lax
from jax.experimental import pallas as pl
from jax.experimental.pallas import tpu as pltpu

D_MODEL = 1024
DEPTH = 4
N_HEADS = 4
HEAD_W = 128
DA_HEAD_DIM = 64
GROUP_W = N_HEADS * HEAD_W
GDN_QKV_W = 3 * GROUP_W
CONV_WIDTH = 4
GDN_CHUNK = 128
GDN_INTRA_UNROLL = 8
N_EXPERTS = 8
TOP_K = 2
D_FF_DENSE = 2816
D_FF_EXPERT = 3584
ROPE_THETA = 10000.0
DEEPNORM_ALPHA = (2.0 * DEPTH) ** 0.25
LN_EPS = 1e-5
RMS_EPS = 1e-6
NEG_BIG = -1e30
LOG2_E = math.log2(math.e)

ROW_TILE = 512
ATTN_TILE = 256
DENSE_FF_TILE = 1408
EXPERT_FF_TILE = 1792
EXPERT_ROWS = 512
MOE_ROW_TILE = 256
VMEM_LIMIT = 56 * 1024 * 1024

_BF16 = jnp.bfloat16
_F32 = jnp.float32


def _cparams(sem):
    return pltpu.CompilerParams(dimension_semantics=sem, vmem_limit_bytes=VMEM_LIMIT)


def _dot(a, b):
    return jnp.dot(a, b, preferred_element_type=_F32)


def _dot_nt(a, b):
    return lax.dot_general(a, b, (((1,), (1,)), ((), ())), preferred_element_type=_F32)


def _dot_hi(a, b):
    return jnp.dot(a, b, preferred_element_type=_F32, precision=lax.Precision.HIGHEST)


def _sigmoid(x):
    return 1.0 / (1.0 + jnp.exp(-x))


def _silu(x):
    return x * _sigmoid(x)


def _layer_norm(y, g, b):
    mu = jnp.mean(y, axis=-1, keepdims=True)
    yc = y - mu
    var = jnp.mean(yc * yc, axis=-1, keepdims=True)
    return yc * lax.rsqrt(var + LN_EPS) * g + b


def _proj_kernel(x_ref, wq_ref, wk_ref, wvt_ref, wg_ref, wz_ref, wab_ref, cos_ref, sin_ref,
                 arow_ref, dtb_ref, q_ref, k_ref, vt_ref, g_ref, z_ref, gb_ref):
    x = x_ref[...]
    tm = x.shape[0]
    cos = jnp.concatenate([cos_ref[...]] * N_HEADS, axis=1)
    sin = jnp.concatenate([sin_ref[...]] * N_HEADS, axis=1)
    lane = lax.broadcasted_iota(jnp.int32, (tm, GROUP_W), 1)
    first_half = (lane % DA_HEAD_DIM) < (DA_HEAD_DIM // 2)

    def rope(t):
        partner = jnp.where(first_half,
                            pltpu.roll(t, GROUP_W - DA_HEAD_DIM // 2, 1),
                            pltpu.roll(t, DA_HEAD_DIM // 2, 1))
        return t * cos + partner * sin

    q_ref[...] = (rope(_dot(x, wq_ref[...])) * (DA_HEAD_DIM ** -0.5 * LOG2_E)).astype(q_ref.dtype)
    k_ref[...] = rope(_dot(x, wk_ref[...])).astype(k_ref.dtype)
    kv_tile = vt_ref.shape[2]
    for t in range(vt_ref.shape[0]):
        vt_ref[t] = _dot_nt(wvt_ref[...], x[t * kv_tile:(t + 1) * kv_tile, :]).astype(vt_ref.dtype)
    g_ref[...] = _dot(x, wg_ref[...])
    z_ref[...] = _dot(x, wz_ref[...])
    ab = _dot(x, wab_ref[...])
    t = ab + dtb_ref[...]
    softplus = jnp.maximum(t, 0.0) + jnp.log1p(jnp.exp(-jnp.abs(t)))
    lane_ab = lax.broadcasted_iota(jnp.int32, ab.shape, 1)
    gb_ref[...] = jnp.where(lane_ab < N_HEADS, -arow_ref[...] * softplus, _sigmoid(ab))


def _proj(xb, w_all, layer, wvt, wab, cos, sin, arow, dtb, kv_tile):
    T = xb.shape[0]
    tm = min(ROW_TILE, T)
    row = lambda i: (i, 0)
    full = lambda i: (0, 0)
    wspec = lambda w: pl.BlockSpec(w.shape, full)
    cols = lambda width, start: pl.BlockSpec((None, D_MODEL, width), lambda i: (layer, 0, start // width))
    return pl.pallas_call(
        _proj_kernel,
        grid=(T // tm,),
        in_specs=[pl.BlockSpec((tm, D_MODEL), row), cols(GROUP_W, 0), cols(GROUP_W, GROUP_W), wspec(wvt),
                  cols(GDN_QKV_W, 3 * GROUP_W), cols(GROUP_W, 3 * GROUP_W + GDN_QKV_W),
                  wspec(wab), pl.BlockSpec((tm, HEAD_W), row),
                  pl.BlockSpec((tm, HEAD_W), row), wspec(arow), wspec(dtb)],
        out_specs=[pl.BlockSpec((tm, GROUP_W), row), pl.BlockSpec((tm, GROUP_W), row),
                   pl.BlockSpec((tm // kv_tile, GROUP_W, kv_tile), lambda i: (i, 0, 0)),
                   pl.BlockSpec((tm, GDN_QKV_W), row),
                   pl.BlockSpec((tm, GROUP_W), row), pl.BlockSpec((tm, HEAD_W), row)],
        out_shape=[jax.ShapeDtypeStruct((T, GROUP_W), _BF16), jax.ShapeDtypeStruct((T, GROUP_W), _BF16),
                   jax.ShapeDtypeStruct((T // kv_tile, GROUP_W, kv_tile), _BF16),
                   jax.ShapeDtypeStruct((T, GDN_QKV_W), _F32),
                   jax.ShapeDtypeStruct((T, GROUP_W), _F32), jax.ShapeDtypeStruct((T, HEAD_W), _F32)],
        compiler_params=_cparams(("arbitrary",)),
        name="in_proj",
    )(xb, w_all, w_all, wvt, w_all, w_all, wab, cos, sin, arow, dtb)


def _attn_kernel(q_ref, k_ref, vt_ref, lamv_ref, subw_ref, o_ref, *, tile, lam_init):
    qi = pl.program_id(1)
    lane = lax.broadcasted_iota(jnp.int32, (tile, HEAD_W), 1)
    qm = []
    for h in range(N_HEADS):
        q = q_ref[:, h * HEAD_W:(h + 1) * HEAD_W]
        zero = jnp.zeros_like(q)
        qm.append(jnp.where(lane < DA_HEAD_DIM, q, zero))
        qm.append(jnp.where(lane >= DA_HEAD_DIM, q, zero))

    def step(kb, carry, masked):
        off = pl.multiple_of(kb * tile, tile)
        J = range(2 * N_HEADS)
        ks = [k_ref[pl.ds(off, tile), h * HEAD_W:(h + 1) * HEAD_W] for h in range(N_HEADS)]
        vts = [vt_ref[kb, h * HEAD_W:(h + 1) * HEAD_W, :] for h in range(N_HEADS)]
        s = [_dot_nt(ks[j // 2], qm[j]) for j in J]
        if masked:
            key = lax.broadcasted_iota(jnp.int32, (tile, tile), 0)
            qry = lax.broadcasted_iota(jnp.int32, (tile, tile), 1)
            s = [jnp.where(qry >= key, t, NEG_BIG) for t in s]
        m_new = [jnp.maximum(carry[j][0], jnp.max(s[j], axis=0, keepdims=True)) for j in J]
        alpha = [jnp.exp2(carry[j][0] - m_new[j]) for j in J]
        p = [jnp.exp2(s[j] - m_new[j]) for j in J]
        l = [alpha[j] * carry[j][1] + jnp.sum(p[j], axis=0, keepdims=True) for j in J]
        pv = [_dot(vts[j // 2], p[j].astype(_BF16)) for j in J]
        return tuple((m_new[j], l[j], alpha[j] * carry[j][2] + pv[j]) for j in J)

    rowv = lambda val: jnp.full((1, tile), val, _F32)
    init = (rowv(NEG_BIG), rowv(0.0), jnp.zeros((HEAD_W, tile), _F32))
    carry = lax.fori_loop(0, qi, lambda kb, c: step(kb, c, False), (init,) * (2 * N_HEADS))
    carry = step(qi, carry, True)

    lamv = lamv_ref[...]
    e1 = jnp.exp(jnp.sum(lamv[0:1, :] * lamv[1:2, :], axis=-1, keepdims=True))
    e2 = jnp.exp(jnp.sum(lamv[2:3, :] * lamv[3:4, :], axis=-1, keepdims=True))
    lam = e1 - e2 + lam_init
    for h in range(N_HEADS):
        (_, l1, a1), (_, l2, a2) = carry[2 * h], carry[2 * h + 1]
        o = a1 / l1 - lam * (a2 / l2)
        o = o * lax.rsqrt(jnp.mean(o * o, axis=0, keepdims=True) + RMS_EPS)
        o_ref[:, h * HEAD_W:(h + 1) * HEAD_W] = (o.T * subw_ref[...] * (1.0 - lam_init)).astype(o_ref.dtype)


def _attention(q, k, vt, lamv, subw, B, S, lam_init):
    tile = min(ATTN_TILE, S)
    nq = S // tile
    return pl.pallas_call(
        functools.partial(_attn_kernel, tile=tile, lam_init=lam_init),
        grid=(B, nq),
        in_specs=[pl.BlockSpec((tile, GROUP_W), lambda b, i: (b * nq + i, 0)),
                  pl.BlockSpec((S, GROUP_W), lambda b, i: (b, 0)),
                  pl.BlockSpec((nq, GROUP_W, tile), lambda b, i: (b, 0, 0)),
                  pl.BlockSpec(lamv.shape, lambda b, i: (0, 0)),
                  pl.BlockSpec(subw.shape, lambda b, i: (0, 0))],
        out_specs=pl.BlockSpec((tile, GROUP_W), lambda b, i: (b * nq + i, 0)),
        out_shape=jax.ShapeDtypeStruct((B * S, GROUP_W), _BF16),
        compiler_params=_cparams(("arbitrary", "arbitrary")),
        name="diff_attn",
    )(q, k, vt, lamv, subw)


def _gdn_kernel(xq_ref, xk_ref, xv_ref, cwq_ref, cwk_ref, cwv_ref, gb_ref, z_ref, nw_ref, o_ref,
                xpad, qs, ks, vs, lhs_s, ob_s, cd_s, out_s, *, seq):
    h = pl.program_id(1)
    C = GDN_CHUNK

    def conv_silu(x_ref, cw_ref):
        xpad[0:8, :] = jnp.zeros((8, HEAD_W), _F32)
        xpad[8:, :] = x_ref[...]
        cw = cw_ref[...]
        acc = xpad[pl.ds(8, seq), :] * cw[3:4, :]
        for j in range(CONV_WIDTH - 1):
            acc = acc + xpad[pl.ds(5 + j, seq), :] * cw[j:j + 1, :]
        return _silu(acc)

    def l2n(t):
        return t * lax.rsqrt(jnp.sum(t * t, axis=-1, keepdims=True) + RMS_EPS)

    qs[...] = l2n(conv_silu(xq_ref, cwq_ref)) * (HEAD_W ** -0.5)
    ks[...] = l2n(conv_silu(xk_ref, cwk_ref))
    vs[...] = conv_silu(xv_ref, cwv_ref)

    ri = lax.broadcasted_iota(jnp.int32, (C, C), 0)
    ci = lax.broadcasted_iota(jnp.int32, (C, C), 1)
    tril = ri >= ci
    strict = ri > ci
    tril_b = tril.astype(_BF16)
    eye_f = (ri == ci).astype(_F32)
    lane = lax.broadcasted_iota(jnp.int32, (C, HEAD_W), 1)
    n_chunks = seq // C
    unroll = math.gcd(n_chunks, GDN_INTRA_UNROLL)

    def intra_group(i, carry):
        cs = [i * unroll + j for j in range(unroll)]
        r0 = [pl.multiple_of(c * C, C) for c in cs]
        r2 = [pl.multiple_of(c * 2 * C, 2 * C) for c in cs]
        U = range(unroll)
        kc = [ks[pl.ds(r, C), :] for r in r0]
        gbs = [gb_ref[pl.ds(r, C), :] for r in r0]
        g_col = [jnp.sum(jnp.where(lane == h, gb, 0.0), axis=-1, keepdims=True) for gb in gbs]
        b_col = [jnp.sum(jnp.where(lane == h + N_HEADS, gb, 0.0), axis=-1, keepdims=True) for gb in gbs]
        g_b = [jnp.broadcast_to(g, (C, C)) for g in g_col]
        g_hi = [g.astype(_BF16) for g in g_b]
        g_r1 = [g_b[j] - g_hi[j].astype(_F32) for j in U]
        g_mid = [g.astype(_BF16) for g in g_r1]
        g_lo = [(g_r1[j] - g_mid[j].astype(_F32)).astype(_BF16) for j in U]
        g_rows = [_dot(tril_b, g_hi[j]) + _dot(tril_b, g_mid[j]) + _dot(tril_b, g_lo[j]) for j in U]
        decay = [jnp.exp(jnp.where(tril, g - g.T, NEG_BIG)) for g in g_rows]
        k_beta = [kc[j] * b_col[j] for j in U]
        kc_b = [k.astype(_BF16) for k in kc]
        kk = [_dot_nt(k_beta[j].astype(_BF16), kc_b[j]) for j in U]
        lmat = [jnp.where(strict, kk[j] * decay[j], 0.0) for j in U]
        inv = [eye_f - l for l in lmat]
        pw = lmat
        for _ in range(int(math.log2(C)) - 1):
            pw_b = [p.astype(_BF16) for p in pw]
            pw = [_dot(p, p) for p in pw_b]
            upd = [_dot(inv[j].astype(_BF16), pw[j].astype(_BF16)) for j in U]
            inv = [inv[j] + upd[j] for j in U]
        l_hi = [l.astype(_BF16) for l in lmat]
        l_lo = [(lmat[j] - l_hi[j].astype(_F32)).astype(_BF16) for j in U]
        x_hi = [x.astype(_BF16) for x in inv]
        x_lo = [(inv[j] - x_hi[j].astype(_F32)).astype(_BF16) for j in U]
        lx = [_dot(l_hi[j], x_hi[j]) + _dot(l_hi[j], x_lo[j]) + _dot(l_lo[j], x_hi[j]) for j in U]
        resid = [eye_f - inv[j] - lx[j] for j in U]
        inv = [inv[j] + _dot(x_hi[j], resid[j].astype(_BF16)) for j in U]
        e_g = [jnp.exp(g) for g in g_rows]
        rhs = [jnp.concatenate([vs[pl.ds(r0[j], C), :] * b_col[j], k_beta[j] * e_g[j]], axis=1).astype(_BF16)
               for j in U]
        sol = [_dot(inv[j].astype(_BF16), rhs[j]).astype(_BF16) for j in U]
        qc = [qs[pl.ds(r, C), :] for r in r0]
        qkm = [_dot_nt(qc[j].astype(_BF16), kc_b[j]) for j in U]
        g_last = [g[C - 1:C, :] for g in g_rows]
        k_tail = [kc[j] * jnp.exp(g_last[j] - g_rows[j]) for j in U]
        lhs = [jnp.concatenate([jnp.where(tril, qkm[j] * decay[j], 0.0), k_tail[j].T], axis=0).astype(_BF16)
               for j in U]
        prod = [_dot(lhs[j], sol[j]) for j in U]
        for j in U:
            ob_s[pl.ds(r2[j], 2 * C), :] = prod[j][:, :HEAD_W]
            lhs_s[pl.ds(r2[j], C), :] = (qc[j] * e_g[j] - prod[j][:C, HEAD_W:]).astype(_BF16)
            lhs_s[pl.ds(r2[j] + C, C), :] = (-prod[j][C:, HEAD_W:]).astype(_BF16)
            cd_s[pl.ds(pl.multiple_of(cs[j] * 8, 8), 8), :] = jnp.broadcast_to(jnp.exp(g_last[j]), (8, HEAD_W))
        return carry

    lax.fori_loop(0, n_chunks // unroll, intra_group, 0)

    def recur(c, state):
        r0 = pl.multiple_of(c * C, C)
        r2 = pl.multiple_of(c * 2 * C, 2 * C)
        res = _dot(lhs_s[pl.ds(r2, 2 * C), :], state.astype(_BF16)) + ob_s[pl.ds(r2, 2 * C), :]
        out_s[pl.ds(r0, C), :] = res[:C, :]
        return state * cd_s[pl.ds(pl.multiple_of(c * 8, 8), 1), :] + res[C:, :]

    lax.fori_loop(0, n_chunks, recur, jnp.zeros((HEAD_W, HEAD_W), _F32))

    out = out_s[...]
    out = out * lax.rsqrt(jnp.mean(out * out, axis=-1, keepdims=True) + RMS_EPS) * nw_ref[...]
    o_ref[...] = (out * _silu(z_ref[...])).astype(o_ref.dtype)


def _gdn(g, conv_w, gb, z, nw, B, S):
    blk = lambda j: pl.BlockSpec((S, HEAD_W), lambda b, h: (b, h + j * N_HEADS))
    cw = lambda j: pl.BlockSpec((CONV_WIDTH, HEAD_W), lambda b, h: (0, h + j * N_HEADS))
    return pl.pallas_call(
        functools.partial(_gdn_kernel, seq=S),
        grid=(B, N_HEADS),
        in_specs=[blk(0), blk(1), blk(2), cw(0), cw(1), cw(2),
                  pl.BlockSpec((S, HEAD_W), lambda b, h: (b, 0)),
                  pl.BlockSpec((S, HEAD_W), lambda b, h: (b, h)),
                  pl.BlockSpec((1, HEAD_W), lambda b, h: (0, 0))],
        out_specs=pl.BlockSpec((S, HEAD_W), lambda b, h: (b, h)),
        out_shape=jax.ShapeDtypeStruct((B * S, GROUP_W), _BF16),
        scratch_shapes=[pltpu.VMEM((S + 8, HEAD_W), _F32), pltpu.VMEM((S, HEAD_W), _F32),
                        pltpu.VMEM((S, HEAD_W), _F32), pltpu.VMEM((S, HEAD_W), _F32),
                        pltpu.VMEM((2 * S, HEAD_W), _BF16),
                        pltpu.VMEM((2 * S, HEAD_W), _F32),
                        pltpu.VMEM((S // GDN_CHUNK * 8, HEAD_W), _F32),
                        pltpu.VMEM((S, HEAD_W), _F32)],
        compiler_params=_cparams(("arbitrary", "arbitrary")),
        name="gdn",
    )(g, g, g, conv_w, conv_w, conv_w, gb, z, nw)


def _outproj_kernel(da_ref, gd_ref, x_ref, wa_ref, wg_ref, g_ref, b_ref, *rest, with_router):
    if with_router:
        rw_ref, xo_ref, xb_ref, lg_ref = rest
    else:
        xo_ref, xb_ref = rest
    hmix = _dot(da_ref[...], wa_ref[...]) + _dot(gd_ref[...], wg_ref[...])
    y = _layer_norm(DEEPNORM_ALPHA * x_ref[...] + hmix, g_ref[...], b_ref[...])
    xo_ref[...] = y
    xb_ref[...] = y.astype(xb_ref.dtype)
    if with_router:
        rw = rw_ref[...]
        y_hi = y.astype(_BF16)
        y_lo = (y - y_hi.astype(_F32)).astype(_BF16)
        rw_hi = rw.astype(_BF16)
        rw_lo = (rw - rw_hi.astype(_F32)).astype(_BF16)
        lg_ref[...] = _dot(y_hi, rw_hi) + (_dot(y_hi, rw_lo) + _dot(y_lo, rw_hi))


def _outproj(da, gd, x, w_all, layer, g, b, rw=None):
    T = x.shape[0]
    tm = min(ROW_TILE, T)
    row = lambda i: (i, 0)
    full = lambda i: (0, 0)
    rows = lambda j: pl.BlockSpec((None, GROUP_W, D_MODEL), lambda i: (layer, j, 0))
    in_specs = [pl.BlockSpec((tm, GROUP_W), row), pl.BlockSpec((tm, GROUP_W), row),
                pl.BlockSpec((tm, D_MODEL), row), rows(0), rows(1),
                pl.BlockSpec(g.shape, full), pl.BlockSpec(b.shape, full)]
    out_specs = [pl.BlockSpec((tm, D_MODEL), row), pl.BlockSpec((tm, D_MODEL), row)]
    out_shape = [jax.ShapeDtypeStruct((T, D_MODEL), _F32), jax.ShapeDtypeStruct((T, D_MODEL), _BF16)]
    args = [da, gd, x, w_all, w_all, g, b]
    if rw is not None:
        in_specs.append(pl.BlockSpec(rw.shape, full))
        out_specs.append(pl.BlockSpec((tm, HEAD_W), row))
        out_shape.append(jax.ShapeDtypeStruct((T, HEAD_W), _F32))
        args.append(rw)
    return pl.pallas_call(
        functools.partial(_outproj_kernel, with_router=rw is not None),
        grid=(T // tm,), in_specs=in_specs, out_specs=out_specs, out_shape=out_shape,
        compiler_params=_cparams(("arbitrary",)),
        name="out_proj_ln",
    )(*args)


def _ffn_kernel(xb_ref, x_ref, wg_ref, wu_ref, wd_ref, g_ref, b_ref, xo_ref, xbo_ref, acc_ref):
    f = pl.program_id(1)
    xb = xb_ref[...]
    hidden = (_silu(_dot(xb, wg_ref[...])) * _dot(xb, wu_ref[...])).astype(_BF16)
    part = _dot(hidden, wd_ref[...])

    @pl.when(f == 0)
    def _():
        acc_ref[...] = part

    @pl.when(f > 0)
    def _():
        acc_ref[...] += part

    @pl.when(f == pl.num_programs(1) - 1)
    def _():
        y = _layer_norm(DEEPNORM_ALPHA * x_ref[...] + acc_ref[...], g_ref[...], b_ref[...])
        xo_ref[...] = y
        xbo_ref[...] = y.astype(xbo_ref.dtype)


def _ffn(xb, x, wg, wu, wd, g, b):
    T = x.shape[0]
    tm = min(ROW_TILE, T)
    tf = DENSE_FF_TILE
    row = lambda i, f: (i, 0)
    full = lambda i, f: (0, 0)
    return pl.pallas_call(
        _ffn_kernel,
        grid=(T // tm, D_FF_DENSE // tf),
        in_specs=[pl.BlockSpec((tm, D_MODEL), row), pl.BlockSpec((tm, D_MODEL), row),
                  pl.BlockSpec((D_MODEL, tf), lambda i, f: (0, f)),
                  pl.BlockSpec((D_MODEL, tf), lambda i, f: (0, f)),
                  pl.BlockSpec((tf, D_MODEL), lambda i, f: (f, 0)),
                  pl.BlockSpec(g.shape, full), pl.BlockSpec(b.shape, full)],
        out_specs=[pl.BlockSpec((tm, D_MODEL), row), pl.BlockSpec((tm, D_MODEL), row)],
        out_shape=[jax.ShapeDtypeStruct((T, D_MODEL), _F32), jax.ShapeDtypeStruct((T, D_MODEL), _BF16)],
        scratch_shapes=[pltpu.VMEM((tm, D_MODEL), _F32)],
        compiler_params=_cparams(("arbitrary", "arbitrary")),
        name="dense_ffn_ln",
    )(xb, x, wg, wu, wd, g, b)


def _dispatch_kernel(dest_ref, x_ref, slots_in, slots_out, sem, *, tm):
    del slots_in

    def issue(r, c):
        for k in range(TOP_K):
            d = dest_ref[0, 0, TOP_K * r + k]
            pltpu.make_async_copy(x_ref.at[pl.ds(r, 1), :], slots_out.at[pl.ds(d, 1), :], sem.at[0]).start()
        return c

    lax.fori_loop(0, tm, issue, 0, unroll=8)
    span = slots_out.at[pl.ds(0, TOP_K * tm), :]
    pltpu.make_async_copy(span, span, sem.at[0]).wait()


def _dispatch(dest3, x, P):
    T = x.shape[0]
    tm = dest3.shape[2] // TOP_K
    return pl.pallas_call(
        functools.partial(_dispatch_kernel, tm=tm),
        grid=(T // tm,),
        in_specs=[pl.BlockSpec((1, 1, TOP_K * tm), lambda i: (i, 0, 0), memory_space=pltpu.SMEM),
                  pl.BlockSpec((tm, D_MODEL), lambda i: (i, 0)), pl.BlockSpec(memory_space=pl.ANY)],
        out_specs=pl.BlockSpec(memory_space=pl.ANY),
        out_shape=jax.ShapeDtypeStruct((P, D_MODEL), _F32),
        scratch_shapes=[pltpu.SemaphoreType.DMA((1,))],
        input_output_aliases={2: 0},
        compiler_params=_cparams(("arbitrary",)),
        name="moe_dispatch",
    )(dest3, x, jnp.zeros((P, D_MODEL), _F32))


def _expert_kernel(meta_ref, xg_ref, wg_ref, wu_ref, wd_ref, y_ref, xb_ref, acc_ref):
    i = pl.program_id(0)
    f = pl.program_id(1)
    last_f = pl.num_programs(1) - 1
    active = i < meta_ref[pl.num_programs(0)]

    @pl.when(active)
    def _():
        @pl.when(f == 0)
        def _():
            xb_ref[...] = xg_ref[...].astype(_BF16)

        xb = xb_ref[...]
        hidden = (_silu(_dot(xb, wg_ref[0])) * _dot(xb, wu_ref[0])).astype(_BF16)
        part = _dot(hidden, wd_ref[0])

        @pl.when(f == 0)
        def _():
            acc_ref[...] = part

        @pl.when(f > 0)
        def _():
            acc_ref[...] += part

        @pl.when(f == last_f)
        def _():
            y_ref[...] = acc_ref[...]

    @pl.when(jnp.logical_and(jnp.logical_not(active), f == last_f))
    def _():
        y_ref[...] = jnp.zeros_like(y_ref)


def _experts(meta, xg, wg, wu, wd):
    P = xg.shape[0]
    tm = EXPERT_ROWS
    tf = EXPERT_FF_TILE
    n_blk = P // tm
    nf = D_FF_EXPERT // tf
    ff = lambda i, f, m: jnp.where(i < m[n_blk], f, nf - 1)
    grid_spec = pltpu.PrefetchScalarGridSpec(
        num_scalar_prefetch=1,
        grid=(n_blk, nf),
        in_specs=[pl.BlockSpec((tm, D_MODEL), lambda i, f, m: (i, 0)),
                  pl.BlockSpec((1, D_MODEL, tf), lambda i, f, m: (m[i], 0, ff(i, f, m))),
                  pl.BlockSpec((1, D_MODEL, tf), lambda i, f, m: (m[i], 0, ff(i, f, m))),
                  pl.BlockSpec((1, tf, D_MODEL), lambda i, f, m: (m[i], ff(i, f, m), 0))],
        out_specs=pl.BlockSpec((tm, D_MODEL), lambda i, f, m: (i, 0)),
        scratch_shapes=[pltpu.VMEM((tm, D_MODEL), _BF16), pltpu.VMEM((tm, D_MODEL), _F32)],
    )
    return pl.pallas_call(
        _expert_kernel, grid_spec=grid_spec,
        out_shape=jax.ShapeDtypeStruct((P, D_MODEL), _F32),
        compiler_params=_cparams(("arbitrary", "arbitrary")),
        name="expert_ffn",
    )(meta, xg, wg, wu, wd)


def _combine_kernel(cur_ref, nxt_ref, y_hbm, gate_ref, x_ref, g_ref, b_ref, xo_ref, xbo_ref, buf, sem, *, tm):
    i = pl.program_id(0)
    slot = i % 2

    def issue(idx_ref, s):
        def body(r, c):
            for k in range(TOP_K):
                d = idx_ref[0, 0, TOP_K * r + k]
                pltpu.make_async_copy(y_hbm.at[pl.ds(d, 1), :], buf.at[s, k, pl.ds(r, 1), :], sem.at[s]).start()
            return c
        lax.fori_loop(0, tm, body, 0, unroll=8)

    @pl.when(i == 0)
    def _():
        issue(cur_ref, 0)

    @pl.when(i + 1 < pl.num_programs(0))
    def _():
        issue(nxt_ref, 1 - slot)

    pltpu.make_async_copy(buf.at[slot], buf.at[slot], sem.at[slot]).wait()
    gates = gate_ref[...]
    f = gates[:, 0:1] * buf[slot, 0] + gates[:, 1:2] * buf[slot, 1]
    y = _layer_norm(DEEPNORM_ALPHA * x_ref[...] + f, g_ref[...], b_ref[...])
    xo_ref[...] = y
    xbo_ref[...] = y.astype(xbo_ref.dtype)


def _combine(dest3, y, gates, x, g, b):
    T = x.shape[0]
    tm = dest3.shape[2] // TOP_K
    n = T // tm
    row = lambda i: (i, 0)
    full = lambda i: (0, 0)
    smem = lambda imap: pl.BlockSpec((1, 1, TOP_K * tm), imap, memory_space=pltpu.SMEM)
    return pl.pallas_call(
        functools.partial(_combine_kernel, tm=tm),
        grid=(n,),
        in_specs=[smem(lambda i: (i, 0, 0)), smem(lambda i: (jnp.minimum(i + 1, n - 1), 0, 0)),
                  pl.BlockSpec(memory_space=pl.ANY), pl.BlockSpec((tm, TOP_K), row),
                  pl.BlockSpec((tm, D_MODEL), row), pl.BlockSpec(g.shape, full), pl.BlockSpec(b.shape, full)],
        out_specs=[pl.BlockSpec((tm, D_MODEL), row), pl.BlockSpec((tm, D_MODEL), row)],
        out_shape=[jax.ShapeDtypeStruct((T, D_MODEL), _F32), jax.ShapeDtypeStruct((T, D_MODEL), _BF16)],
        scratch_shapes=[pltpu.VMEM((2, TOP_K, tm, D_MODEL), _F32), pltpu.SemaphoreType.DMA((2,))],
        compiler_params=_cparams(("arbitrary",)),
        name="moe_combine_ln",
    )(dest3, dest3, y, gates, x, g, b)


def _route(logits, T):
    eid = jnp.arange(N_EXPERTS, dtype=jnp.int32)[None, :]
    i1 = jnp.argmax(logits, axis=-1).astype(jnp.int32)
    v1 = jnp.max(logits, axis=-1)
    rest = jnp.where(eid == i1[:, None], -jnp.inf, logits)
    i2 = jnp.argmax(rest, axis=-1).astype(jnp.int32)
    v2 = jnp.max(rest, axis=-1)
    e2 = jnp.exp(v2 - v1)
    gates = jnp.stack([1.0 / (1.0 + e2), e2 / (1.0 + e2)], axis=-1)
    e = jnp.stack([i1, i2], axis=-1).reshape(T * TOP_K)
    onehot = (e[:, None] == eid).astype(jnp.int32)
    rank = jnp.sum((jnp.cumsum(onehot, axis=0) - onehot) * onehot, axis=-1)
    counts = jnp.sum(onehot, axis=0)
    padded = ((counts + EXPERT_ROWS - 1) // EXPERT_ROWS) * EXPERT_ROWS
    pend = jnp.cumsum(padded)
    dest = jnp.sum(onehot * (pend - padded)[None, :], axis=-1) + rank
    n_blk = (T * TOP_K) // EXPERT_ROWS + N_EXPERTS
    blk_start = jnp.arange(n_blk, dtype=jnp.int32) * EXPERT_ROWS
    blk_e = jnp.minimum(jnp.sum((blk_start[:, None] >= pend[None, :]).astype(jnp.int32), axis=-1), N_EXPERTS - 1)
    meta = jnp.concatenate([blk_e, pend[-1:] // EXPERT_ROWS]).astype(jnp.int32)
    return gates, dest.astype(jnp.int32), meta


def _moe(x, logits, wg, wu, wd, g, b):
    T = x.shape[0]
    gates, dest, meta = _route(logits[:, :N_EXPERTS], T)
    tm = min(MOE_ROW_TILE, T)
    dest3 = dest.reshape(T // tm, 1, TOP_K * tm)
    P = T * TOP_K + N_EXPERTS * EXPERT_ROWS
    xg = _dispatch(dest3, x, P)
    y = _experts(meta, xg, wg, wu, wd)
    return _combine(dest3, y, gates, x, g, b)


def kernel(x, positions, w_in, conv_w, a_log, dt_bias, gdn_norm_w, lam_q1, lam_k1, lam_q2, lam_k2, subln_w, w_out, ln1_g, ln1_b, ln2_g, ln2_b, ffn_w_gate, ffn_w_up, ffn_w_down, router_w, moe_w_gate, moe_w_up, moe_w_down):
    B, S, _ = x.shape
    T = B * S
    half = DA_HEAD_DIM // 2
    inv_freq = ROPE_THETA ** (-jnp.arange(0, DA_HEAD_DIM, 2, dtype=_F32) / DA_HEAD_DIM)
    ang = positions.astype(_F32).reshape(T, 1) * inv_freq[None, :]
    cos = jnp.tile(jnp.cos(ang), (1, HEAD_W // half))
    sin = jnp.tile(jnp.concatenate([-jnp.sin(ang), jnp.sin(ang)], axis=1), (1, HEAD_W // DA_HEAD_DIM))

    v0, v1 = 2 * GROUP_W, 3 * GROUP_W
    ab0 = 4 * GROUP_W + GDN_QKV_W
    xf = x.reshape(T, D_MODEL)
    xb = xf.astype(_BF16)
    row = lambda v: v.reshape(1, -1).astype(_F32)
    pad_lanes = lambda v: jnp.pad(v.astype(_F32), (0, HEAD_W - v.shape[0])).reshape(1, HEAD_W)
    w_in_b = w_in.astype(_BF16)
    w_out_b = w_out.astype(_BF16)

    for l in range(DEPTH):
        lam_init = 0.8 - 0.6 * math.exp(-0.3 * l)
        wvt = w_in_b[l, :, v0:v1].T
        wab = jnp.pad(w_in_b[l, :, ab0:ab0 + 2 * N_HEADS], ((0, 0), (0, HEAD_W - 2 * N_HEADS)))
        q, k, vt, g, z, gb = _proj(xb, w_in_b, l, wvt, wab, cos, sin, pad_lanes(jnp.exp(a_log[l])),
                                   pad_lanes(dt_bias[l]), min(ATTN_TILE, S))
        lamv = jnp.stack([lam_q1[l], lam_k1[l], lam_q2[l], lam_k2[l]]).astype(_F32)
        da = _attention(q, k, vt, lamv, row(subln_w[l]), B, S, lam_init)
        gd = _gdn(g, conv_w[l].astype(_F32), gb, z, row(gdn_norm_w[l]), B, S)
        if l % 2 == 0:
            xf, xb = _outproj(da, gd, xf, w_out_b, l, row(ln1_g[l]), row(ln1_b[l]))
            i = l // 2
            xf, xb = _ffn(xb, xf, ffn_w_gate[i].astype(_BF16), ffn_w_up[i].astype(_BF16),
                          ffn_w_down[i].astype(_BF16), row(ln2_g[l]), row(ln2_b[l]))
        else:
            i = l // 2
            rw = jnp.pad(router_w[i].astype(_F32), ((0, 0), (0, HEAD_W - N_EXPERTS)))
            xf, xb, logits = _outproj(da, gd, xf, w_out_b, l, row(ln1_g[l]), row(ln1_b[l]), rw)
            xf, xb = _moe(xf, logits, moe_w_gate[i].astype(_BF16), moe_w_up[i].astype(_BF16),
                          moe_w_down[i].astype(_BF16), row(ln2_g[l]), row(ln2_b[l]))
    return xf.reshape(B, S, D_MODEL)
```

```python
import functools
import math

import jax
import jax.numpy as jnp
from jax import lax
from jax.experimental import pallas as pl
from jax.experimental.pallas import tpu as pltpu

D_MODEL = 1024
DEPTH = 4
N_HEADS = 4
HEAD_W = 128
DA_HEAD_DIM = 64
GROUP_W = N_HEADS * HEAD_W
GDN_QKV_W = 3 * GROUP_W
CONV_WIDTH = 4
GDN_CHUNK = 128
GDN_INTRA_UNROLL = 8
N_EXPERTS = 8
TOP_K = 2
D_FF_DENSE = 2816
D_FF_EXPERT = 3584
ROPE_THETA = 10000.0
DEEPNORM_ALPHA = (2.0 * DEPTH) ** 0.25
LN_EPS = 1e-5
RMS_EPS = 1e-6
NEG_BIG = -1e30
LOG2_E = math.log2(math.e)

ROW_TILE = 512
ATTN_TILE = 256
DENSE_FF_TILE = 1408
EXPERT_FF_TILE = 1792
EXPERT_ROWS = 512
MOE_ROW_TILE = 256
VMEM_LIMIT = 56 * 1024 * 1024

_BF16 = jnp.bfloat16
_F32 = jnp.float32


def _cparams(sem):
    return pltpu.CompilerParams(dimension_semantics=sem, vmem_limit_bytes=VMEM_LIMIT)


def _dot(a, b):
    return jnp.dot(a, b, preferred_element_type=_F32)


def _dot_nt(a, b):
    return lax.dot_general(a, b, (((1,), (1,)), ((), ())), preferred_element_type=_F32)


def _dot_hi(a, b):
    return jnp.dot(a, b, preferred_element_type=_F32, precision=lax.Precision.HIGHEST)


def _sigmoid(x):
    return 1.0 / (1.0 + jnp.exp(-x))


def _silu(x):
    return x * _sigmoid(x)


def _layer_norm(y, g, b):
    mu = jnp.mean(y, axis=-1, keepdims=True)
    yc = y - mu
    var = jnp.mean(yc * yc, axis=-1, keepdims=True)
    return yc * lax.rsqrt(var + LN_EPS) * g + b


def _proj_kernel(x_ref, wq_ref, wk_ref, wvt_ref, wg_ref, wz_ref, wab_ref, cos_ref, sin_ref,
                 arow_ref, dtb_ref, q_ref, k_ref, vt_ref, g_ref, z_ref, gb_ref):
    x = x_ref[...]
    tm = x.shape[0]
    cos = jnp.concatenate([cos_ref[...]] * N_HEADS, axis=1)
    sin = jnp.concatenate([sin_ref[...]] * N_HEADS, axis=1)
    lane = lax.broadcasted_iota(jnp.int32, (tm, GROUP_W), 1)
    first_half = (lane % DA_HEAD_DIM) < (DA_HEAD_DIM // 2)

    def rope(t):
        partner = jnp.where(first_half,
                            pltpu.roll(t, GROUP_W - DA_HEAD_DIM // 2, 1),
                            pltpu.roll(t, DA_HEAD_DIM // 2, 1))
        return t * cos + partner * sin

    q_ref[...] = (rope(_dot(x, wq_ref[...])) * (DA_HEAD_DIM ** -0.5 * LOG2_E)).astype(q_ref.dtype)
    k_ref[...] = rope(_dot(x, wk_ref[...])).astype(k_ref.dtype)
    kv_tile = vt_ref.shape[2]
    for t in range(vt_ref.shape[0]):
        vt_ref[t] = _dot_nt(wvt_ref[...], x[t * kv_tile:(t + 1) * kv_tile, :]).astype(vt_ref.dtype)
    g_ref[...] = _dot(x, wg_ref[...])
    z_ref[...] = _dot(x, wz_ref[...])
    ab = _dot(x, wab_ref[...])
    t = ab + dtb_ref[...]
    softplus = jnp.maximum(t, 0.0) + jnp.log1p(jnp.exp(-jnp.abs(t)))
    lane_ab = lax.broadcasted_iota(jnp.int32, ab.shape, 1)
    gb_ref[...] = jnp.where(lane_ab < N_HEADS, -arow_ref[...] * softplus, _sigmoid(ab))


def _proj(xb, w_all, layer, wvt, wab, cos, sin, arow, dtb, kv_tile):
    T = xb.shape[0]
    tm = min(ROW_TILE, T)
    row = lambda i: (i, 0)
    full = lambda i: (0, 0)
    wspec = lambda w: pl.BlockSpec(w.shape, full)
    cols = lambda width, start: pl.BlockSpec((None, D_MODEL, width), lambda i: (layer, 0, start // width))
    return pl.pallas_call(
        _proj_kernel,
        grid=(T // tm,),
        in_specs=[pl.BlockSpec((tm, D_MODEL), row), cols(GROUP_W, 0), cols(GROUP_W, GROUP_W), wspec(wvt),
                  cols(GDN_QKV_W, 3 * GROUP_W), cols(GROUP_W, 3 * GROUP_W + GDN_QKV_W),
                  wspec(wab), pl.BlockSpec((tm, HEAD_W), row),
                  pl.BlockSpec((tm, HEAD_W), row), wspec(arow), wspec(dtb)],
        out_specs=[pl.BlockSpec((tm, GROUP_W), row), pl.BlockSpec((tm, GROUP_W), row),
                   pl.BlockSpec((tm // kv_tile, GROUP_W, kv_tile), lambda i: (i, 0, 0)),
                   pl.BlockSpec((tm, GDN_QKV_W), row),
                   pl.BlockSpec((tm, GROUP_W), row), pl.BlockSpec((tm, HEAD_W), row)],
        out_shape=[jax.ShapeDtypeStruct((T, GROUP_W), _BF16), jax.ShapeDtypeStruct((T, GROUP_W), _BF16),
                   jax.ShapeDtypeStruct((T // kv_tile, GROUP_W, kv_tile), _BF16),
                   jax.ShapeDtypeStruct((T, GDN_QKV_W), _F32),
                   jax.ShapeDtypeStruct((T, GROUP_W), _F32), jax.ShapeDtypeStruct((T, HEAD_W), _F32)],
        compiler_params=_cparams(("arbitrary",)),
        name="in_proj",
    )(xb, w_all, w_all, wvt, w_all, w_all, wab, cos, sin, arow, dtb)


def _attn_kernel(q_ref, k_ref, vt_ref, lamv_ref, subw_ref, o_ref, *, tile, lam_init):
    qi = pl.program_id(1)
    lane = lax.broadcasted_iota(jnp.int32, (tile, HEAD_W), 1)
    qm = []
    for h in range(N_HEADS):
        q = q_ref[:, h * HEAD_W:(h + 1) * HEAD_W]
        zero = jnp.zeros_like(q)
        qm.append(jnp.where(lane < DA_HEAD_DIM, q, zero))
        qm.append(jnp.where(lane >= DA_HEAD_DIM, q, zero))

    def step(kb, carry, masked):
        off = pl.multiple_of(kb * tile, tile)
        J = range(2 * N_HEADS)
        ks = [k_ref[pl.ds(off, tile), h * HEAD_W:(h + 1) * HEAD_W] for h in range(N_HEADS)]
        vts = [vt_ref[kb, h * HEAD_W:(h + 1) * HEAD_W, :] for h in range(N_HEADS)]
        s = [_dot_nt(ks[j // 2], qm[j]) for j in J]
        if masked:
            key = lax.broadcasted_iota(jnp.int32, (tile, tile), 0)
            qry = lax.broadcasted_iota(jnp.int32, (tile, tile), 1)
            s = [jnp.where(qry >= key, t, NEG_BIG) for t in s]
        m_new = [jnp.maximum(carry[j][0], jnp.max(s[j], axis=0, keepdims=True)) for j in J]
        alpha = [jnp.exp2(carry[j][0] - m_new[j]) for j in J]
        p = [jnp.exp2(s[j] - m_new[j]) for j in J]
        l = [alpha[j] * carry[j][1] + jnp.sum(p[j], axis=0, keepdims=True) for j in J]
        pv = [_dot(vts[j // 2], p[j].astype(_BF16)) for j in J]
        return tuple((m_new[j], l[j], alpha[j] * carry[j][2] + pv[j]) for j in J)

    rowv = lambda val: jnp.full((1, tile), val, _F32)
    init = (rowv(NEG_BIG), rowv(0.0), jnp.zeros((HEAD_W, tile), _F32))
    carry = lax.fori_loop(0, qi, lambda kb, c: step(kb, c, False), (init,) * (2 * N_HEADS))
    carry = step(qi, carry, True)

    lamv = lamv_ref[...]
    e1 = jnp.exp(jnp.sum(lamv[0:1, :] * lamv[1:2, :], axis=-1, keepdims=True))
    e2 = jnp.exp(jnp.sum(lamv[2:3, :] * lamv[3:4, :], axis=-1, keepdims=True))
    lam = e1 - e2 + lam_init
    for h in range(N_HEADS):
        (_, l1, a1), (_, l2, a2) = carry[2 * h], carry[2 * h + 1]
        o = a1 / l1 - lam * (a2 / l2)
        o = o * lax.rsqrt(jnp.mean(o * o, axis=0, keepdims=True) + RMS_EPS)
        o_ref[:, h * HEAD_W:(h + 1) * HEAD_W] = (o.T * subw_ref[...] * (1.0 - lam_init)).astype(o_ref.dtype)


def _attention(q, k, vt, lamv, subw, B, S, lam_init):
    tile = min(ATTN_TILE, S)
    nq = S // tile
    return pl.pallas_call(
        functools.partial(_attn_kernel, tile=tile, lam_init=lam_init),
        grid=(B, nq),
        in_specs=[pl.BlockSpec((tile, GROUP_W), lambda b, i: (b * nq + i, 0)),
                  pl.BlockSpec((S, GROUP_W), lambda b, i: (b, 0)),
                  pl.BlockSpec((nq, GROUP_W, tile), lambda b, i: (b, 0, 0)),
                  pl.BlockSpec(lamv.shape, lambda b, i: (0, 0)),
                  pl.BlockSpec(subw.shape, lambda b, i: (0, 0))],
        out_specs=pl.BlockSpec((tile, GROUP_W), lambda b, i: (b * nq + i, 0)),
        out_shape=jax.ShapeDtypeStruct((B * S, GROUP_W), _BF16),
        compiler_params=_cparams(("arbitrary", "arbitrary")),
        name="diff_attn",
    )(q, k, vt, lamv, subw)


def _gdn_kernel(xq_ref, xk_ref, xv_ref, cwq_ref, cwk_ref, cwv_ref, gb_ref, z_ref, nw_ref, o_ref,
                xpad, qs, ks, vs, lhs_s, ob_s, cd_s, out_s, *, seq):
    h = pl.program_id(1)
    C = GDN_CHUNK

    def conv_silu(x_ref, cw_ref):
        xpad[0:8, :] = jnp.zeros((8, HEAD_W), _F32)
        xpad[8:, :] = x_ref[...]
        cw = cw_ref[...]
        acc = xpad[pl.ds(8, seq), :] * cw[3:4, :]
        for j in range(CONV_WIDTH - 1):
            acc = acc + xpad[pl.ds(5 + j, seq), :] * cw[j:j + 1, :]
        return _silu(acc)

    def l2n(t):
        return t * lax.rsqrt(jnp.sum(t * t, axis=-1, keepdims=True) + RMS_EPS)

    qs[...] = l2n(conv_silu(xq_ref, cwq_ref)) * (HEAD_W ** -0.5)
    ks[...] = l2n(conv_silu(xk_ref, cwk_ref))
    vs[...] = conv_silu(xv_ref, cwv_ref)

    ri = lax.broadcasted_iota(jnp.int32, (C, C), 0)
    ci = lax.broadcasted_iota(jnp.int32, (C, C), 1)
    tril = ri >= ci
    strict = ri > ci
    tril_b = tril.astype(_BF16)
    eye_f = (ri == ci).astype(_F32)
    lane = lax.broadcasted_iota(jnp.int32, (C, HEAD_W), 1)
    n_chunks = seq // C
    unroll = math.gcd(n_chunks, GDN_INTRA_UNROLL)

    def intra_group(i, carry):
        cs = [i * unroll + j for j in range(unroll)]
        r0 = [pl.multiple_of(c * C, C) for c in cs]
        r2 = [pl.multiple_of(c * 2 * C, 2 * C) for c in cs]
        U = range(unroll)
        kc = [ks[pl.ds(r, C), :] for r in r0]
        gbs = [gb_ref[pl.ds(r, C), :] for r in r0]
        g_col = [jnp.sum(jnp.where(lane == h, gb, 0.0), axis=-1, keepdims=True) for gb in gbs]
        b_col = [jnp.sum(jnp.where(lane == h + N_HEADS, gb, 0.0), axis=-1, keepdims=True) for gb in gbs]
        g_b = [jnp.broadcast_to(g, (C, C)) for g in g_col]
        g_hi = [g.astype(_BF16) for g in g_b]
        g_r1 = [g_b[j] - g_hi[j].astype(_F32) for j in U]
        g_mid = [g.astype(_BF16) for g in g_r1]
        g_lo = [(g_r1[j] - g_mid[j].astype(_F32)).astype(_BF16) for j in U]
        g_rows = [_dot(tril_b, g_hi[j]) + _dot(tril_b, g_mid[j]) + _dot(tril_b, g_lo[j]) for j in U]
        decay = [jnp.exp(jnp.where(tril, g - g.T, NEG_BIG)) for g in g_rows]
        k_beta = [kc[j] * b_col[j] for j in U]
        kc_b = [k.astype(_BF16) for k in kc]
        kk = [_dot_nt(k_beta[j].astype(_BF16), kc_b[j]) for j in U]
        lmat = [jnp.where(strict, kk[j] * decay[j], 0.0) for j in U]
        inv = [eye_f - l for l in lmat]
        pw = lmat
        for _ in range(int(math.log2(C)) - 1):
            pw_b = [p.astype(_BF16) for p in pw]
            pw = [_dot(p, p) for p in pw_b]
            upd = [_dot(inv[j].astype(_BF16), pw[j].astype(_BF16)) for j in U]
            inv = [inv[j] + upd[j] for j in U]
        l_hi = [l.astype(_BF16) for l in lmat]
        l_lo = [(lmat[j] - l_hi[j].astype(_F32)).astype(_BF16) for j in U]
        x_hi = [x.astype(_BF16) for x in inv]
        x_lo = [(inv[j] - x_hi[j].astype(_F32)).astype(_BF16) for j in U]
        lx = [_dot(l_hi[j], x_hi[j]) + _dot(l_hi[j], x_lo[j]) + _dot(l_lo[j], x_hi[j]) for j in U]
        resid = [eye_f - inv[j] - lx[j] for j in U]
        inv = [inv[j] + _dot(x_hi[j], resid[j].astype(_BF16)) for j in U]
        e_g = [jnp.exp(g) for g in g_rows]
        rhs = [jnp.concatenate([vs[pl.ds(r0[j], C), :] * b_col[j], k_beta[j] * e_g[j]], axis=1).astype(_BF16)
               for j in U]
        sol = [_dot(inv[j].astype(_BF16), rhs[j]).astype(_BF16) for j in U]
        qc = [qs[pl.ds(r, C), :] for r in r0]
        qkm = [_dot_nt(qc[j].astype(_BF16), kc_b[j]) for j in U]
        g_last = [g[C - 1:C, :] for g in g_rows]
        k_tail = [kc[j] * jnp.exp(g_last[j] - g_rows[j]) for j in U]
        lhs = [jnp.concatenate([jnp.where(tril, qkm[j] * decay[j], 0.0), k_tail[j].T], axis=0).astype(_BF16)
               for j in U]
        prod = [_dot(lhs[j], sol[j]) for j in U]
        for j in U:
            ob_s[pl.ds(r2[j], 2 * C), :] = prod[j][:, :HEAD_W]
            lhs_s[pl.ds(r2[j], C), :] = (qc[j] * e_g[j] - prod[j][:C, HEAD_W:]).astype(_BF16)
            lhs_s[pl.ds(r2[j] + C, C), :] = (-prod[j][C:, HEAD_W:]).astype(_BF16)
            cd_s[pl.ds(pl.multiple_of(cs[j] * 8, 8), 8), :] = jnp.broadcast_to(jnp.exp(g_last[j]), (8, HEAD_W))
        return carry

    lax.fori_loop(0, n_chunks // unroll, intra_group, 0)

    def recur(c, state):
        r0 = pl.multiple_of(c * C, C)
        r2 = pl.multiple_of(c * 2 * C, 2 * C)
        res = _dot(lhs_s[pl.ds(r2, 2 * C), :], state.astype(_BF16)) + ob_s[pl.ds(r2, 2 * C), :]
        out_s[pl.ds(r0, C), :] = res[:C, :]
        return state * cd_s[pl.ds(pl.multiple_of(c * 8, 8), 1), :] + res[C:, :]

    lax.fori_loop(0, n_chunks, recur, jnp.zeros((HEAD_W, HEAD_W), _F32))

    out = out_s[...]
    out = out * lax.rsqrt(jnp.mean(out * out, axis=-1, keepdims=True) + RMS_EPS) * nw_ref[...]
    o_ref[...] = (out * _silu(z_ref[...])).astype(o_ref.dtype)


def _gdn(g, conv_w, gb, z, nw, B, S):
    blk = lambda j: pl.BlockSpec((S, HEAD_W), lambda b, h: (b, h + j * N_HEADS))
    cw = lambda j: pl.BlockSpec((CONV_WIDTH, HEAD_W), lambda b, h: (0, h + j * N_HEADS))
    return pl.pallas_call(
        functools.partial(_gdn_kernel, seq=S),
        grid=(B, N_HEADS),
        in_specs=[blk(0), blk(1), blk(2), cw(0), cw(1), cw(2),
                  pl.BlockSpec((S, HEAD_W), lambda b, h: (b, 0)),
                  pl.BlockSpec((S, HEAD_W), lambda b, h: (b, h)),
                  pl.BlockSpec((1, HEAD_W), lambda b, h: (0, 0))],
        out_specs=pl.BlockSpec((S, HEAD_W), lambda b, h: (b, h)),
        out_shape=jax.ShapeDtypeStruct((B * S, GROUP_W), _BF16),
        scratch_shapes=[pltpu.VMEM((S + 8, HEAD_W), _F32), pltpu.VMEM((S, HEAD_W), _F32),
                        pltpu.VMEM((S, HEAD_W), _F32), pltpu.VMEM((S, HEAD_W), _F32),
                        pltpu.VMEM((2 * S, HEAD_W), _BF16),
                        pltpu.VMEM((2 * S, HEAD_W), _F32),
                        pltpu.VMEM((S // GDN_CHUNK * 8, HEAD_W), _F32),
                        pltpu.VMEM((S, HEAD_W), _F32)],
        compiler_params=_cparams(("arbitrary", "arbitrary")),
        name="gdn",
    )(g, g, g, conv_w, conv_w, conv_w, gb, z, nw)


def _outproj_kernel(da_ref, gd_ref, x_ref, wa_ref, wg_ref, g_ref, b_ref, *rest, with_router):
    if with_router:
        rw_ref, xo_ref, xb_ref, lg_ref = rest
    else:
        xo_ref, xb_ref = rest
    hmix = _dot(da_ref[...], wa_ref[...]) + _dot(gd_ref[...], wg_ref[...])
    y = _layer_norm(DEEPNORM_ALPHA * x_ref[...] + hmix, g_ref[...], b_ref[...])
    xo_ref[...] = y
    xb_ref[...] = y.astype(xb_ref.dtype)
    if with_router:
        rw = rw_ref[...]
        y_hi = y.astype(_BF16)
        y_lo = (y - y_hi.astype(_F32)).astype(_BF16)
        rw_hi = rw.astype(_BF16)
        rw_lo = (rw - rw_hi.astype(_F32)).astype(_BF16)
        lg_ref[...] = _dot(y_hi, rw_hi) + (_dot(y_hi, rw_lo) + _dot(y_lo, rw_hi))


def _outproj(da, gd, x, w_all, layer, g, b, rw=None):
    T = x.shape[0]
    tm = min(ROW_TILE, T)
    row = lambda i: (i, 0)
    full = lambda i: (0, 0)
    rows = lambda j: pl.BlockSpec((None, GROUP_W, D_MODEL), lambda i: (layer, j, 0))
    in_specs = [pl.BlockSpec((tm, GROUP_W), row), pl.BlockSpec((tm, GROUP_W), row),
                pl.BlockSpec((tm, D_MODEL), row), rows(0), rows(1),
                pl.BlockSpec(g.shape, full), pl.BlockSpec(b.shape, full)]
    out_specs = [pl.BlockSpec((tm, D_MODEL), row), pl.BlockSpec((tm, D_MODEL), row)]
    out_shape = [jax.ShapeDtypeStruct((T, D_MODEL), _F32), jax.ShapeDtypeStruct((T, D_MODEL), _BF16)]
    args = [da, gd, x, w_all, w_all, g, b]
    if rw is not None:
        in_specs.append(pl.BlockSpec(rw.shape, full))
        out_specs.append(pl.BlockSpec((tm, HEAD_W), row))
        out_shape.append(jax.ShapeDtypeStruct((T, HEAD_W), _F32))
        args.append(rw)
    return pl.pallas_call(
        functools.partial(_outproj_kernel, with_router=rw is not None),
        grid=(T // tm,), in_specs=in_specs, out_specs=out_specs, out_shape=out_shape,
        compiler_params=_cparams(("arbitrary",)),
        name="out_proj_ln",
    )(*args)


def _ffn_kernel(xb_ref, x_ref, wg_ref, wu_ref, wd_ref, g_ref, b_ref, xo_ref, xbo_ref, acc_ref):
    f = pl.program_id(1)
    xb = xb_ref[...]
    hidden = (_silu(_dot(xb, wg_ref[...])) * _dot(xb, wu_ref[...])).astype(_BF16)
    part = _dot(hidden, wd_ref[...])

    @pl.when(f == 0)
    def _():
        acc_ref[...] = part

    @pl.when(f > 0)
    def _():
        acc_ref[...] += part

    @pl.when(f == pl.num_programs(1) - 1)
    def _():
        y = _layer_norm(DEEPNORM_ALPHA * x_ref[...] + acc_ref[...], g_ref[...], b_ref[...])
        xo_ref[...] = y
        xbo_ref[...] = y.astype(xbo_ref.dtype)


def _ffn(xb, x, wg, wu, wd, g, b):
    T = x.shape[0]
    tm = min(ROW_TILE, T)
    tf = DENSE_FF_TILE
    row = lambda i, f: (i, 0)
    full = lambda i, f: (0, 0)
    return pl.pallas_call(
        _ffn_kernel,
        grid=(T // tm, D_FF_DENSE // tf),
        in_specs=[pl.BlockSpec((tm, D_MODEL), row), pl.BlockSpec((tm, D_MODEL), row),
                  pl.BlockSpec((D_MODEL, tf), lambda i, f: (0, f)),
                  pl.BlockSpec((D_MODEL, tf), lambda i, f: (0, f)),
                  pl.BlockSpec((tf, D_MODEL), lambda i, f: (f, 0)),
                  pl.BlockSpec(g.shape, full), pl.BlockSpec(b.shape, full)],
        out_specs=[pl.BlockSpec((tm, D_MODEL), row), pl.BlockSpec((tm, D_MODEL), row)],
        out_shape=[jax.ShapeDtypeStruct((T, D_MODEL), _F32), jax.ShapeDtypeStruct((T, D_MODEL), _BF16)],
        scratch_shapes=[pltpu.VMEM((tm, D_MODEL), _F32)],
        compiler_params=_cparams(("arbitrary", "arbitrary")),
        name="dense_ffn_ln",
    )(xb, x, wg, wu, wd, g, b)


def _dispatch_kernel(dest_ref, x_ref, slots_in, slots_out, sem, *, tm):
    del slots_in

    def issue(r, c):
        for k in range(TOP_K):
            d = dest_ref[0, 0, TOP_K * r + k]
            pltpu.make_async_copy(x_ref.at[pl.ds(r, 1), :], slots_out.at[pl.ds(d, 1), :], sem.at[0]).start()
        return c

    lax.fori_loop(0, tm, issue, 0, unroll=8)
    span = slots_out.at[pl.ds(0, TOP_K * tm), :]
    pltpu.make_async_copy(span, span, sem.at[0]).wait()


def _dispatch(dest3, x, P):
    T = x.shape[0]
    tm = dest3.shape[2] // TOP_K
    return pl.pallas_call(
        functools.partial(_dispatch_kernel, tm=tm),
        grid=(T // tm,),
        in_specs=[pl.BlockSpec((1, 1, TOP_K * tm), lambda i: (i, 0, 0), memory_space=pltpu.SMEM),
                  pl.BlockSpec((tm, D_MODEL), lambda i: (i, 0)), pl.BlockSpec(memory_space=pl.ANY)],
        out_specs=pl.BlockSpec(memory_space=pl.ANY),
        out_shape=jax.ShapeDtypeStruct((P, D_MODEL), _F32),
        scratch_shapes=[pltpu.SemaphoreType.DMA((1,))],
        input_output_aliases={2: 0},
        compiler_params=_cparams(("arbitrary",)),
        name="moe_dispatch",
    )(dest3, x, jnp.zeros((P, D_MODEL), _F32))


def _expert_kernel(meta_ref, xg_ref, wg_ref, wu_ref, wd_ref, y_ref, xb_ref, acc_ref):
    i = pl.program_id(0)
    f = pl.program_id(1)
    last_f = pl.num_programs(1) - 1
    active = i < meta_ref[pl.num_programs(0)]

    @pl.when(active)
    def _():
        @pl.when(f == 0)
        def _():
            xb_ref[...] = xg_ref[...].astype(_BF16)

        xb = xb_ref[...]
        hidden = (_silu(_dot(xb, wg_ref[0])) * _dot(xb, wu_ref[0])).astype(_BF16)
        part = _dot(hidden, wd_ref[0])

        @pl.when(f == 0)
        def _():
            acc_ref[...] = part

        @pl.when(f > 0)
        def _():
            acc_ref[...] += part

        @pl.when(f == last_f)
        def _():
            y_ref[...] = acc_ref[...]

    @pl.when(jnp.logical_and(jnp.logical_not(active), f == last_f))
    def _():
        y_ref[...] = jnp.zeros_like(y_ref)


def _experts(meta, xg, wg, wu, wd, layer):
    P = xg.shape[0]
    tm = EXPERT_ROWS
    tf = EXPERT_FF_TILE
    n_blk = P // tm
    nf = D_FF_EXPERT // tf
    ff = lambda i, f, m: jnp.where(i < m[n_blk], f, nf - 1)
    grid_spec = pltpu.PrefetchScalarGridSpec(
        num_scalar_prefetch=1,
        grid=(n_blk, nf),
        in_specs=[pl.BlockSpec((tm, D_MODEL), lambda i, f, m: (i, 0)),
                  pl.BlockSpec((None, 1, D_MODEL, tf), lambda i, f, m: (layer, m[i], 0, ff(i, f, m))),
                  pl.BlockSpec((None, 1, D_MODEL, tf), lambda i, f, m: (layer, m[i], 0, ff(i, f, m))),
                  pl.BlockSpec((None, 1, tf, D_MODEL), lambda i, f, m: (layer, m[i], ff(i, f, m), 0))],
        out_specs=pl.BlockSpec((tm, D_MODEL), lambda i, f, m: (i, 0)),
        scratch_shapes=[pltpu.VMEM((tm, D_MODEL), _BF16), pltpu.VMEM((tm, D_MODEL), _F32)],
    )
    return pl.pallas_call(
        _expert_kernel, grid_spec=grid_spec,
        out_shape=jax.ShapeDtypeStruct((P, D_MODEL), _F32),
        compiler_params=_cparams(("arbitrary", "arbitrary")),
        name="expert_ffn",
    )(meta, xg, wg, wu, wd)


def _combine_kernel(cur_ref, nxt_ref, y_hbm, gate_ref, x_ref, g_ref, b_ref, xo_ref, xbo_ref, buf, sem, *, tm):
    i = pl.program_id(0)
    slot = i % 2

    def issue(idx_ref, s):
        def body(r, c):
            for k in range(TOP_K):
                d = idx_ref[0, 0, TOP_K * r + k]
                pltpu.make_async_copy(y_hbm.at[pl.ds(d, 1), :], buf.at[s, k, pl.ds(r, 1), :], sem.at[s]).start()
            return c
        lax.fori_loop(0, tm, body, 0, unroll=8)

    @pl.when(i == 0)
    def _():
        issue(cur_ref, 0)

    @pl.when(i + 1 < pl.num_programs(0))
    def _():
        issue(nxt_ref, 1 - slot)

    pltpu.make_async_copy(buf.at[slot], buf.at[slot], sem.at[slot]).wait()
    gates = gate_ref[...]
    f = gates[:, 0:1] * buf[slot, 0] + gates[:, 1:2] * buf[slot, 1]
    y = _layer_norm(DEEPNORM_ALPHA * x_ref[...] + f, g_ref[...], b_ref[...])
    xo_ref[...] = y
    xbo_ref[...] = y.astype(xbo_ref.dtype)


def _combine(dest3, y, gates, x, g, b):
    T = x.shape[0]
    tm = dest3.shape[2] // TOP_K
    n = T // tm
    row = lambda i: (i, 0)
    full = lambda i: (0, 0)
    smem = lambda imap: pl.BlockSpec((1, 1, TOP_K * tm), imap, memory_space=pltpu.SMEM)
    return pl.pallas_call(
        functools.partial(_combine_kernel, tm=tm),
        grid=(n,),
        in_specs=[smem(lambda i: (i, 0, 0)), smem(lambda i: (jnp.minimum(i + 1, n - 1), 0, 0)),
                  pl.BlockSpec(memory_space=pl.ANY), pl.BlockSpec((tm, TOP_K), row),
                  pl.BlockSpec((tm, D_MODEL), row), pl.BlockSpec(g.shape, full), pl.BlockSpec(b.shape, full)],
        out_specs=[pl.BlockSpec((tm, D_MODEL), row), pl.BlockSpec((tm, D_MODEL), row)],
        out_shape=[jax.ShapeDtypeStruct((T, D_MODEL), _F32), jax.ShapeDtypeStruct((T, D_MODEL), _BF16)],
        scratch_shapes=[pltpu.VMEM((2, TOP_K, tm, D_MODEL), _F32), pltpu.SemaphoreType.DMA((2,))],
        compiler_params=_cparams(("arbitrary",)),
        name="moe_combine_ln",
    )(dest3, dest3, y, gates, x, g, b)


def _route(logits, T):
    eid = jnp.arange(N_EXPERTS, dtype=jnp.int32)[None, :]
    i1 = jnp.argmax(logits, axis=-1).astype(jnp.int32)
    v1 = jnp.max(logits, axis=-1)
    rest = jnp.where(eid == i1[:, None], -jnp.inf, logits)
    i2 = jnp.argmax(rest, axis=-1).astype(jnp.int32)
    v2 = jnp.max(rest, axis=-1)
    e2 = jnp.exp(v2 - v1)
    gates = jnp.stack([1.0 / (1.0 + e2), e2 / (1.0 + e2)], axis=-1)
    e = jnp.stack([i1, i2], axis=-1).reshape(T * TOP_K)
    onehot = (e[:, None] == eid).astype(jnp.int32)
    rank = jnp.sum((jnp.cumsum(onehot, axis=0) - onehot) * onehot, axis=-1)
    counts = jnp.sum(onehot, axis=0)
    padded = ((counts + EXPERT_ROWS - 1) // EXPERT_ROWS) * EXPERT_ROWS
    pend = jnp.cumsum(padded)
    dest = jnp.sum(onehot * (pend - padded)[None, :], axis=-1) + rank
    n_blk = (T * TOP_K) // EXPERT_ROWS + N_EXPERTS
    blk_start = jnp.arange(n_blk, dtype=jnp.int32) * EXPERT_ROWS
    blk_e = jnp.minimum(jnp.sum((blk_start[:, None] >= pend[None, :]).astype(jnp.int32), axis=-1), N_EXPERTS - 1)
    meta = jnp.concatenate([blk_e, pend[-1:] // EXPERT_ROWS]).astype(jnp.int32)
    return gates, dest.astype(jnp.int32), meta


def _moe(x, logits, wg, wu, wd, layer, g, b):
    T = x.shape[0]
    gates, dest, meta = _route(logits[:, :N_EXPERTS], T)
    tm = min(MOE_ROW_TILE, T)
    dest3 = dest.reshape(T // tm, 1, TOP_K * tm)
    P = T * TOP_K + N_EXPERTS * EXPERT_ROWS
    xg = _dispatch(dest3, x, P)
    y = _experts(meta, xg, wg, wu, wd, layer)
    return _combine(dest3, y, gates, x, g, b)


def kernel(x, positions, w_in, conv_w, a_log, dt_bias, gdn_norm_w, lam_q1, lam_k1, lam_q2, lam_k2, subln_w, w_out, ln1_g, ln1_b, ln2_g, ln2_b, ffn_w_gate, ffn_w_up, ffn_w_down, router_w, moe_w_gate, moe_w_up, moe_w_down):
    B, S, _ = x.shape
    T = B * S
    half = DA_HEAD_DIM // 2
    inv_freq = ROPE_THETA ** (-jnp.arange(0, DA_HEAD_DIM, 2, dtype=_F32) / DA_HEAD_DIM)
    ang = positions.astype(_F32).reshape(T, 1) * inv_freq[None, :]
    cos = jnp.tile(jnp.cos(ang), (1, HEAD_W // half))
    sin = jnp.tile(jnp.concatenate([-jnp.sin(ang), jnp.sin(ang)], axis=1), (1, HEAD_W // DA_HEAD_DIM))

    v0, v1 = 2 * GROUP_W, 3 * GROUP_W
    ab0 = 4 * GROUP_W + GDN_QKV_W
    xf = x.reshape(T, D_MODEL)
    xb = xf.astype(_BF16)
    row = lambda v: v.reshape(1, -1).astype(_F32)
    pad_lanes = lambda v: jnp.pad(v.astype(_F32), (0, HEAD_W - v.shape[0])).reshape(1, HEAD_W)
    w_in_b = w_in.astype(_BF16)
    w_out_b = w_out.astype(_BF16)
    moe_gate_b, moe_up_b, moe_down_b = (w.astype(_BF16) for w in (moe_w_gate, moe_w_up, moe_w_down))

    for l in range(DEPTH):
        lam_init = 0.8 - 0.6 * math.exp(-0.3 * l)
        wvt = w_in_b[l, :, v0:v1].T
        wab = jnp.pad(w_in_b[l, :, ab0:ab0 + 2 * N_HEADS], ((0, 0), (0, HEAD_W - 2 * N_HEADS)))
        q, k, vt, g, z, gb = _proj(xb, w_in_b, l, wvt, wab, cos, sin, pad_lanes(jnp.exp(a_log[l])),
                                   pad_lanes(dt_bias[l]), min(ATTN_TILE, S))
        lamv = jnp.stack([lam_q1[l], lam_k1[l], lam_q2[l], lam_k2[l]]).astype(_F32)
        da = _attention(q, k, vt, lamv, row(subln_w[l]), B, S, lam_init)
        gd = _gdn(g, conv_w[l].astype(_F32), gb, z, row(gdn_norm_w[l]), B, S)
        if l % 2 == 0:
            xf, xb = _outproj(da, gd, xf, w_out_b, l, row(ln1_g[l]), row(ln1_b[l]))
            i = l // 2
            xf, xb = _ffn(xb, xf, ffn_w_gate[i].astype(_BF16), ffn_w_up[i].astype(_BF16),
                          ffn_w_down[i].astype(_BF16), row(ln2_g[l]), row(ln2_b[l]))
        else:
            i = l // 2
            rw = jnp.pad(router_w[i].astype(_F32), ((0, 0), (0, HEAD_W - N_EXPERTS)))
            xf, xb, logits = _outproj(da, gd, xf, w_out_b, l, row(ln1_g[l]), row(ln1_b[l]), rw)
            xf, xb = _moe(xf, logits, moe_gate_b, moe_up_b, moe_down_b, i, row(ln2_g[l]), row(ln2_b[l]))
    return xf.reshape(B, S, D_MODEL)
```

```python
import functools
import math

import jax
import jax.numpy as jnp
from jax import lax
from jax.experimental import pallas as pl
from jax.experimental.pallas import tpu as pltpu

D_MODEL = 1024
DEPTH = 4
N_HEADS = 4
HEAD_W = 128
DA_HEAD_DIM = 64
GROUP_W = N_HEADS * HEAD_W
GDN_QKV_W = 3 * GROUP_W
CONV_WIDTH = 4
GDN_CHUNK = 128
GDN_INTRA_UNROLL = 8
N_EXPERTS = 8
TOP_K = 2
D_FF_DENSE = 2816
D_FF_EXPERT = 3584
ROPE_THETA = 10000.0
DEEPNORM_ALPHA = (2.0 * DEPTH) ** 0.25
LN_EPS = 1e-5
RMS_EPS = 1e-6
NEG_BIG = -1e30
LOG2_E = math.log2(math.e)

ROW_TILE = 512
ATTN_TILE = 256
DENSE_FF_TILE = 1408
EXPERT_FF_TILE = 1792
EXPERT_ROWS = 512
MOE_ROW_TILE = 256
VMEM_LIMIT = 56 * 1024 * 1024

_BF16 = jnp.bfloat16
_F32 = jnp.float32


def _cparams(sem):
    return pltpu.CompilerParams(dimension_semantics=sem, vmem_limit_bytes=VMEM_LIMIT)


def _dot(a, b):
    return jnp.dot(a, b, preferred_element_type=_F32)


def _dot_nt(a, b):
    return lax.dot_general(a, b, (((1,), (1,)), ((), ())), preferred_element_type=_F32)


def _dot_hi(a, b):
    return jnp.dot(a, b, preferred_element_type=_F32, precision=lax.Precision.HIGHEST)


def _sigmoid(x):
    return 1.0 / (1.0 + jnp.exp(-x))


def _silu(x):
    return x * _sigmoid(x)


def _layer_norm(y, g, b):
    mu = jnp.mean(y, axis=-1, keepdims=True)
    yc = y - mu
    var = jnp.mean(yc * yc, axis=-1, keepdims=True)
    return yc * lax.rsqrt(var + LN_EPS) * g + b


def _proj_kernel(x_ref, wq_ref, wk_ref, wvt_ref, wg_ref, wz_ref, wab_ref, cos_ref, sin_ref,
                 arow_ref, dtb_ref, q_ref, k_ref, vt_ref, g_ref, z_ref, gb_ref):
    x = x_ref[...]
    tm = x.shape[0]
    cos = jnp.concatenate([cos_ref[...]] * N_HEADS, axis=1)
    sin = jnp.concatenate([sin_ref[...]] * N_HEADS, axis=1)
    lane = lax.broadcasted_iota(jnp.int32, (tm, GROUP_W), 1)
    first_half = (lane % DA_HEAD_DIM) < (DA_HEAD_DIM // 2)

    def rope(t):
        partner = jnp.where(first_half,
                            pltpu.roll(t, GROUP_W - DA_HEAD_DIM // 2, 1),
                            pltpu.roll(t, DA_HEAD_DIM // 2, 1))
        return t * cos + partner * sin

    q_ref[...] = (rope(_dot(x, wq_ref[...])) * (DA_HEAD_DIM ** -0.5 * LOG2_E)).astype(q_ref.dtype)
    k_ref[...] = rope(_dot(x, wk_ref[...])).astype(k_ref.dtype)
    kv_tile = vt_ref.shape[2]
    for t in range(vt_ref.shape[0]):
        vt_ref[t] = _dot_nt(wvt_ref[...], x[t * kv_tile:(t + 1) * kv_tile, :]).astype(vt_ref.dtype)
    g_ref[...] = _dot(x, wg_ref[...])
    z_ref[...] = _dot(x, wz_ref[...])
    ab = _dot(x, wab_ref[...])
    t = ab + dtb_ref[...]
    softplus = jnp.maximum(t, 0.0) + jnp.log1p(jnp.exp(-jnp.abs(t)))
    lane_ab = lax.broadcasted_iota(jnp.int32, ab.shape, 1)
    gb_ref[...] = jnp.where(lane_ab < N_HEADS, -arow_ref[...] * softplus, _sigmoid(ab))


def _proj(xb, w_all, layer, wvt, wab, cos, sin, arow, dtb, kv_tile):
    T = xb.shape[0]
    tm = min(ROW_TILE, T)
    row = lambda i: (i, 0)
    full = lambda i: (0, 0)
    wspec = lambda w: pl.BlockSpec(w.shape, full)
    cols = lambda width, start: pl.BlockSpec((None, D_MODEL, width), lambda i: (layer, 0, start // width))
    return pl.pallas_call(
        _proj_kernel,
        grid=(T // tm,),
        in_specs=[pl.BlockSpec((tm, D_MODEL), row), cols(GROUP_W, 0), cols(GROUP_W, GROUP_W), wspec(wvt),
                  cols(GDN_QKV_W, 3 * GROUP_W), cols(GROUP_W, 3 * GROUP_W + GDN_QKV_W),
                  wspec(wab), pl.BlockSpec((tm, HEAD_W), row),
                  pl.BlockSpec((tm, HEAD_W), row), wspec(arow), wspec(dtb)],
        out_specs=[pl.BlockSpec((tm, GROUP_W), row), pl.BlockSpec((tm, GROUP_W), row),
                   pl.BlockSpec((tm // kv_tile, GROUP_W, kv_tile), lambda i: (i, 0, 0)),
                   pl.BlockSpec((tm, GDN_QKV_W), row),
                   pl.BlockSpec((tm, GROUP_W), row), pl.BlockSpec((tm, HEAD_W), row)],
        out_shape=[jax.ShapeDtypeStruct((T, GROUP_W), _BF16), jax.ShapeDtypeStruct((T, GROUP_W), _BF16),
                   jax.ShapeDtypeStruct((T // kv_tile, GROUP_W, kv_tile), _BF16),
                   jax.ShapeDtypeStruct((T, GDN_QKV_W), _F32),
                   jax.ShapeDtypeStruct((T, GROUP_W), _F32), jax.ShapeDtypeStruct((T, HEAD_W), _F32)],
        compiler_params=_cparams(("arbitrary",)),
        name="in_proj",
    )(xb, w_all, w_all, wvt, w_all, w_all, wab, cos, sin, arow, dtb)


def _attn_kernel(q_ref, k_ref, vt_ref, lamv_ref, subw_ref, o_ref, *, tile, lam_init):
    qi = pl.program_id(1)
    lane = lax.broadcasted_iota(jnp.int32, (tile, HEAD_W), 1)
    qm = []
    for h in range(N_HEADS):
        q = q_ref[:, h * HEAD_W:(h + 1) * HEAD_W]
        zero = jnp.zeros_like(q)
        qm.append(jnp.where(lane < DA_HEAD_DIM, q, zero))
        qm.append(jnp.where(lane >= DA_HEAD_DIM, q, zero))

    def step(kb, carry, masked):
        off = pl.multiple_of(kb * tile, tile)
        J = range(2 * N_HEADS)
        ks = [k_ref[pl.ds(off, tile), h * HEAD_W:(h + 1) * HEAD_W] for h in range(N_HEADS)]
        vts = [vt_ref[kb, h * HEAD_W:(h + 1) * HEAD_W, :] for h in range(N_HEADS)]
        s = [_dot_nt(ks[j // 2], qm[j]) for j in J]
        if masked:
            key = lax.broadcasted_iota(jnp.int32, (tile, tile), 0)
            qry = lax.broadcasted_iota(jnp.int32, (tile, tile), 1)
            s = [jnp.where(qry >= key, t, NEG_BIG) for t in s]
        m_new = [jnp.maximum(carry[j][0], jnp.max(s[j], axis=0, keepdims=True)) for j in J]
        alpha = [jnp.exp2(carry[j][0] - m_new[j]) for j in J]
        p = [jnp.exp2(s[j] - m_new[j]) for j in J]
        l = [alpha[j] * carry[j][1] + jnp.sum(p[j], axis=0, keepdims=True) for j in J]
        pv = [_dot(vts[j // 2], p[j].astype(_BF16)) for j in J]
        return tuple((m_new[j], l[j], alpha[j] * carry[j][2] + pv[j]) for j in J)

    rowv = lambda val: jnp.full((1, tile), val, _F32)
    init = (rowv(NEG_BIG), rowv(0.0), jnp.zeros((HEAD_W, tile), _F32))
    carry = lax.fori_loop(0, qi, lambda kb, c: step(kb, c, False), (init,) * (2 * N_HEADS))
    carry = step(qi, carry, True)

    lamv = lamv_ref[...]
    e1 = jnp.exp(jnp.sum(lamv[0:1, :] * lamv[1:2, :], axis=-1, keepdims=True))
    e2 = jnp.exp(jnp.sum(lamv[2:3, :] * lamv[3:4, :], axis=-1, keepdims=True))
    lam = e1 - e2 + lam_init
    for h in range(N_HEADS):
        (_, l1, a1), (_, l2, a2) = carry[2 * h], carry[2 * h + 1]
        o = a1 / l1 - lam * (a2 / l2)
        o = o * lax.rsqrt(jnp.mean(o * o, axis=0, keepdims=True) + RMS_EPS)
        o_ref[:, h * HEAD_W:(h + 1) * HEAD_W] = (o.T * subw_ref[...] * (1.0 - lam_init)).astype(o_ref.dtype)


def _attention(q, k, vt, lamv, subw, B, S, lam_init):
    tile = min(ATTN_TILE, S)
    nq = S // tile
    return pl.pallas_call(
        functools.partial(_attn_kernel, tile=tile, lam_init=lam_init),
        grid=(B, nq),
        in_specs=[pl.BlockSpec((tile, GROUP_W), lambda b, i: (b * nq + i, 0)),
                  pl.BlockSpec((S, GROUP_W), lambda b, i: (b, 0)),
                  pl.BlockSpec((nq, GROUP_W, tile), lambda b, i: (b, 0, 0)),
                  pl.BlockSpec(lamv.shape, lambda b, i: (0, 0)),
                  pl.BlockSpec(subw.shape, lambda b, i: (0, 0))],
        out_specs=pl.BlockSpec((tile, GROUP_W), lambda b, i: (b * nq + i, 0)),
        out_shape=jax.ShapeDtypeStruct((B * S, GROUP_W), _BF16),
        compiler_params=_cparams(("arbitrary", "arbitrary")),
        name="diff_attn",
    )(q, k, vt, lamv, subw)


def _gdn_kernel(xq_ref, xk_ref, xv_ref, cwq_ref, cwk_ref, cwv_ref, gb_ref, z_ref, nw_ref, o_ref,
                xpad, qs, ks, vs, lhs_s, ob_s, cd_s, out_s, *, seq):
    h = pl.program_id(1)
    C = GDN_CHUNK

    def conv_silu(x_ref, cw_ref):
        xpad[0:8, :] = jnp.zeros((8, HEAD_W), _F32)
        xpad[8:, :] = x_ref[...]
        cw = cw_ref[...]
        acc = xpad[pl.ds(8, seq), :] * cw[3:4, :]
        for j in range(CONV_WIDTH - 1):
            acc = acc + xpad[pl.ds(5 + j, seq), :] * cw[j:j + 1, :]
        return _silu(acc)

    def l2n(t):
        return t * lax.rsqrt(jnp.sum(t * t, axis=-1, keepdims=True) + RMS_EPS)

    qs[...] = l2n(conv_silu(xq_ref, cwq_ref)) * (HEAD_W ** -0.5)
    ks[...] = l2n(conv_silu(xk_ref, cwk_ref))
    vs[...] = conv_silu(xv_ref, cwv_ref)

    ri = lax.broadcasted_iota(jnp.int32, (C, C), 0)
    ci = lax.broadcasted_iota(jnp.int32, (C, C), 1)
    tril = ri >= ci
    strict = ri > ci
    tril_b = tril.astype(_BF16)
    eye_f = (ri == ci).astype(_F32)
    lane = lax.broadcasted_iota(jnp.int32, (C, HEAD_W), 1)
    n_chunks = seq // C
    unroll = math.gcd(n_chunks, GDN_INTRA_UNROLL)

    def intra_group(i, carry):
        cs = [i * unroll + j for j in range(unroll)]
        r0 = [pl.multiple_of(c * C, C) for c in cs]
        r2 = [pl.multiple_of(c * 2 * C, 2 * C) for c in cs]
        U = range(unroll)
        kc = [ks[pl.ds(r, C), :] for r in r0]
        gbs = [gb_ref[pl.ds(r, C), :] for r in r0]
        g_col = [jnp.sum(jnp.where(lane == h, gb, 0.0), axis=-1, keepdims=True) for gb in gbs]
        b_col = [jnp.sum(jnp.where(lane == h + N_HEADS, gb, 0.0), axis=-1, keepdims=True) for gb in gbs]
        g_b = [jnp.broadcast_to(g, (C, C)) for g in g_col]
        g_hi = [g.astype(_BF16) for g in g_b]
        g_r1 = [g_b[j] - g_hi[j].astype(_F32) for j in U]
        g_mid = [g.astype(_BF16) for g in g_r1]
        g_lo = [(g_r1[j] - g_mid[j].astype(_F32)).astype(_BF16) for j in U]
        g_rows = [_dot(tril_b, g_hi[j]) + _dot(tril_b, g_mid[j]) + _dot(tril_b, g_lo[j]) for j in U]
        decay = [jnp.exp(jnp.where(tril, g - g.T, NEG_BIG)) for g in g_rows]
        k_beta = [kc[j] * b_col[j] for j in U]
        kc_b = [k.astype(_BF16) for k in kc]
        kk = [_dot_nt(k_beta[j].astype(_BF16), kc_b[j]) for j in U]
        lmat = [jnp.where(strict, kk[j] * decay[j], 0.0) for j in U]
        inv = [eye_f - l for l in lmat]
        pw = lmat
        for _ in range(int(math.log2(C)) - 1):
            pw_b = [p.astype(_BF16) for p in pw]
            pw = [_dot(p, p) for p in pw_b]
            upd = [_dot(inv[j].astype(_BF16), pw[j].astype(_BF16)) for j in U]
            inv = [inv[j] + upd[j] for j in U]
        l_hi = [l.astype(_BF16) for l in lmat]
        l_lo = [(lmat[j] - l_hi[j].astype(_F32)).astype(_BF16) for j in U]
        x_hi = [x.astype(_BF16) for x in inv]
        x_lo = [(inv[j] - x_hi[j].astype(_F32)).astype(_BF16) for j in U]
        lx = [_dot(l_hi[j], x_hi[j]) + _dot(l_hi[j], x_lo[j]) + _dot(l_lo[j], x_hi[j]) for j in U]
        resid = [eye_f - inv[j] - lx[j] for j in U]
        inv = [inv[j] + _dot(x_hi[j], resid[j].astype(_BF16)) for j in U]
        e_g = [jnp.exp(g) for g in g_rows]
        rhs = [jnp.concatenate([vs[pl.ds(r0[j], C), :] * b_col[j], k_beta[j] * e_g[j]], axis=1).astype(_BF16)
               for j in U]
        sol = [_dot(inv[j].astype(_BF16), rhs[j]).astype(_BF16) for j in U]
        qc = [qs[pl.ds(r, C), :] for r in r0]
        qkm = [_dot_nt(qc[j].astype(_BF16), kc_b[j]) for j in U]
        g_last = [g[C - 1:C, :] for g in g_rows]
        k_tail = [kc[j] * jnp.exp(g_last[j] - g_rows[j]) for j in U]
        lhs = [jnp.concatenate([jnp.where(tril, qkm[j] * decay[j], 0.0), k_tail[j].T], axis=0).astype(_BF16)
               for j in U]
        prod = [_dot(lhs[j], sol[j]) for j in U]
        for j in U:
            ob_s[pl.ds(r2[j], 2 * C), :] = prod[j][:, :HEAD_W]
            lhs_s[pl.ds(r2[j], C), :] = (qc[j] * e_g[j] - prod[j][:C, HEAD_W:]).astype(_BF16)
            lhs_s[pl.ds(r2[j] + C, C), :] = (-prod[j][C:, HEAD_W:]).astype(_BF16)
            cd_s[pl.ds(pl.multiple_of(cs[j] * 8, 8), 8), :] = jnp.broadcast_to(jnp.exp(g_last[j]), (8, HEAD_W))
        return carry

    lax.fori_loop(0, n_chunks // unroll, intra_group, 0)

    def recur(c, state):
        r0 = pl.multiple_of(c * C, C)
        r2 = pl.multiple_of(c * 2 * C, 2 * C)
        res = _dot(lhs_s[pl.ds(r2, 2 * C), :], state.astype(_BF16)) + ob_s[pl.ds(r2, 2 * C), :]
        out_s[pl.ds(r0, C), :] = res[:C, :]
        return state * cd_s[pl.ds(pl.multiple_of(c * 8, 8), 1), :] + res[C:, :]

    lax.fori_loop(0, n_chunks, recur, jnp.zeros((HEAD_W, HEAD_W), _F32))

    out = out_s[...]
    out = out * lax.rsqrt(jnp.mean(out * out, axis=-1, keepdims=True) + RMS_EPS) * nw_ref[...]
    o_ref[...] = (out * _silu(z_ref[...])).astype(o_ref.dtype)


def _gdn(g, conv_w, gb, z, nw, B, S):
    blk = lambda j: pl.BlockSpec((S, HEAD_W), lambda b, h: (b, h + j * N_HEADS))
    cw = lambda j: pl.BlockSpec((CONV_WIDTH, HEAD_W), lambda b, h: (0, h + j * N_HEADS))
    return pl.pallas_call(
        functools.partial(_gdn_kernel, seq=S),
        grid=(B, N_HEADS),
        in_specs=[blk(0), blk(1), blk(2), cw(0), cw(1), cw(2),
                  pl.BlockSpec((S, HEAD_W), lambda b, h: (b, 0)),
                  pl.BlockSpec((S, HEAD_W), lambda b, h: (b, h)),
                  pl.BlockSpec((1, HEAD_W), lambda b, h: (0, 0))],
        out_specs=pl.BlockSpec((S, HEAD_W), lambda b, h: (b, h)),
        out_shape=jax.ShapeDtypeStruct((B * S, GROUP_W), _BF16),
        scratch_shapes=[pltpu.VMEM((S + 8, HEAD_W), _F32), pltpu.VMEM((S, HEAD_W), _F32),
                        pltpu.VMEM((S, HEAD_W), _F32), pltpu.VMEM((S, HEAD_W), _F32),
                        pltpu.VMEM((2 * S, HEAD_W), _BF16),
                        pltpu.VMEM((2 * S, HEAD_W), _F32),
                        pltpu.VMEM((S // GDN_CHUNK * 8, HEAD_W), _F32),
                        pltpu.VMEM((S, HEAD_W), _F32)],
        compiler_params=_cparams(("arbitrary", "arbitrary")),
        name="gdn",
    )(g, g, g, conv_w, conv_w, conv_w, gb, z, nw)


def _outproj_kernel(da_ref, gd_ref, x_ref, wa_ref, wg_ref, g_ref, b_ref, *rest, with_router):
    if with_router:
        rw_ref, xo_ref, xb_ref, lg_ref = rest
    else:
        xo_ref, xb_ref = rest
    hmix = _dot(da_ref[...], wa_ref[...]) + _dot(gd_ref[...], wg_ref[...])
    y = _layer_norm(DEEPNORM_ALPHA * x_ref[...] + hmix, g_ref[...], b_ref[...])
    xo_ref[...] = y
    xb_ref[...] = y.astype(xb_ref.dtype)
    if with_router:
        rw = rw_ref[...]
        y_hi = y.astype(_BF16)
        y_lo = (y - y_hi.astype(_F32)).astype(_BF16)
        rw_hi = rw.astype(_BF16)
        rw_lo = (rw - rw_hi.astype(_F32)).astype(_BF16)
        lg_ref[...] = _dot(y_hi, rw_hi) + (_dot(y_hi, rw_lo) + _dot(y_lo, rw_hi))


def _outproj(da, gd, x, w_all, layer, g, b, rw=None):
    T = x.shape[0]
    tm = min(ROW_TILE, T)
    row = lambda i: (i, 0)
    full = lambda i: (0, 0)
    rows = lambda j: pl.BlockSpec((None, GROUP_W, D_MODEL), lambda i: (layer, j, 0))
    in_specs = [pl.BlockSpec((tm, GROUP_W), row), pl.BlockSpec((tm, GROUP_W), row),
                pl.BlockSpec((tm, D_MODEL), row), rows(0), rows(1),
                pl.BlockSpec(g.shape, full), pl.BlockSpec(b.shape, full)]
    out_specs = [pl.BlockSpec((tm, D_MODEL), row), pl.BlockSpec((tm, D_MODEL), row)]
    out_shape = [jax.ShapeDtypeStruct((T, D_MODEL), _F32), jax.ShapeDtypeStruct((T, D_MODEL), _BF16)]
    args = [da, gd, x, w_all, w_all, g, b]
    if rw is not None:
        in_specs.append(pl.BlockSpec(rw.shape, full))
        out_specs.append(pl.BlockSpec((tm, HEAD_W), row))
        out_shape.append(jax.ShapeDtypeStruct((T, HEAD_W), _F32))
        args.append(rw)
    return pl.pallas_call(
        functools.partial(_outproj_kernel, with_router=rw is not None),
        grid=(T // tm,), in_specs=in_specs, out_specs=out_specs, out_shape=out_shape,
        compiler_params=_cparams(("arbitrary",)),
        name="out_proj_ln",
    )(*args)


def _ffn_kernel(xb_ref, x_ref, wg_ref, wu_ref, wd_ref, g_ref, b_ref, xo_ref, xbo_ref, acc_ref):
    f = pl.program_id(1)
    xb = xb_ref[...]
    hidden = (_silu(_dot(xb, wg_ref[...])) * _dot(xb, wu_ref[...])).astype(_BF16)
    part = _dot(hidden, wd_ref[...])

    @pl.when(f == 0)
    def _():
        acc_ref[...] = part

    @pl.when(f > 0)
    def _():
        acc_ref[...] += part

    @pl.when(f == pl.num_programs(1) - 1)
    def _():
        y = _layer_norm(DEEPNORM_ALPHA * x_ref[...] + acc_ref[...], g_ref[...], b_ref[...])
        xo_ref[...] = y
        xbo_ref[...] = y.astype(xbo_ref.dtype)


def _ffn(xb, x, wg, wu, wd, g, b):
    T = x.shape[0]
    tm = min(ROW_TILE, T)
    tf = DENSE_FF_TILE
    row = lambda i, f: (i, 0)
    full = lambda i, f: (0, 0)
    return pl.pallas_call(
        _ffn_kernel,
        grid=(T // tm, D_FF_DENSE // tf),
        in_specs=[pl.BlockSpec((tm, D_MODEL), row), pl.BlockSpec((tm, D_MODEL), row),
                  pl.BlockSpec((D_MODEL, tf), lambda i, f: (0, f)),
                  pl.BlockSpec((D_MODEL, tf), lambda i, f: (0, f)),
                  pl.BlockSpec((tf, D_MODEL), lambda i, f: (f, 0)),
                  pl.BlockSpec(g.shape, full), pl.BlockSpec(b.shape, full)],
        out_specs=[pl.BlockSpec((tm, D_MODEL), row), pl.BlockSpec((tm, D_MODEL), row)],
        out_shape=[jax.ShapeDtypeStruct((T, D_MODEL), _F32), jax.ShapeDtypeStruct((T, D_MODEL), _BF16)],
        scratch_shapes=[pltpu.VMEM((tm, D_MODEL), _F32)],
        compiler_params=_cparams(("arbitrary", "arbitrary")),
        name="dense_ffn_ln",
    )(xb, x, wg, wu, wd, g, b)


def _dispatch_kernel(dest_ref, x_ref, slots_in, slots_out, sem, *, tm):
    del slots_in

    def issue(r, c):
        for k in range(TOP_K):
            d = dest_ref[0, 0, TOP_K * r + k]
            pltpu.make_async_copy(x_ref.at[pl.ds(r, 1), :], slots_out.at[pl.ds(d, 1), :],
                                  sem.at[0]).start(priority=k % 2)
        return c

    lax.fori_loop(0, tm, issue, 0, unroll=8)
    span = slots_out.at[pl.ds(0, TOP_K * tm), :]
    pltpu.make_async_copy(span, span, sem.at[0]).wait()


def _dispatch(dest3, x, P):
    T = x.shape[0]
    tm = dest3.shape[2] // TOP_K
    return pl.pallas_call(
        functools.partial(_dispatch_kernel, tm=tm),
        grid=(T // tm,),
        in_specs=[pl.BlockSpec((1, 1, TOP_K * tm), lambda i: (i, 0, 0), memory_space=pltpu.SMEM),
                  pl.BlockSpec((tm, D_MODEL), lambda i: (i, 0)), pl.BlockSpec(memory_space=pl.ANY)],
        out_specs=pl.BlockSpec(memory_space=pl.ANY),
        out_shape=jax.ShapeDtypeStruct((P, D_MODEL), _F32),
        scratch_shapes=[pltpu.SemaphoreType.DMA((1,))],
        input_output_aliases={2: 0},
        compiler_params=_cparams(("arbitrary",)),
        name="moe_dispatch",
    )(dest3, x, jnp.zeros((P, D_MODEL), _F32))


def _expert_kernel(meta_ref, xg_ref, wg_ref, wu_ref, wd_ref, y_ref, xb_ref, acc_ref):
    i = pl.program_id(0)
    f = pl.program_id(1)
    last_f = pl.num_programs(1) - 1
    active = i < meta_ref[pl.num_programs(0)]

    @pl.when(active)
    def _():
        @pl.when(f == 0)
        def _():
            xb_ref[...] = xg_ref[...].astype(_BF16)

        xb = xb_ref[...]
        hidden = (_silu(_dot(xb, wg_ref[0])) * _dot(xb, wu_ref[0])).astype(_BF16)
        part = _dot(hidden, wd_ref[0])

        @pl.when(f == 0)
        def _():
            acc_ref[...] = part

        @pl.when(f > 0)
        def _():
            acc_ref[...] += part

        @pl.when(f == last_f)
        def _():
            y_ref[...] = acc_ref[...]

    @pl.when(jnp.logical_and(jnp.logical_not(active), f == last_f))
    def _():
        y_ref[...] = jnp.zeros_like(y_ref)


def _experts(meta, xg, wg, wu, wd, layer):
    P = xg.shape[0]
    tm = EXPERT_ROWS
    tf = EXPERT_FF_TILE
    n_blk = P // tm
    nf = D_FF_EXPERT // tf
    ff = lambda i, f, m: jnp.where(i < m[n_blk], f, nf - 1)
    grid_spec = pltpu.PrefetchScalarGridSpec(
        num_scalar_prefetch=1,
        grid=(n_blk, nf),
        in_specs=[pl.BlockSpec((tm, D_MODEL), lambda i, f, m: (i, 0)),
                  pl.BlockSpec((None, 1, D_MODEL, tf), lambda i, f, m: (layer, m[i], 0, ff(i, f, m))),
                  pl.BlockSpec((None, 1, D_MODEL, tf), lambda i, f, m: (layer, m[i], 0, ff(i, f, m))),
                  pl.BlockSpec((None, 1, tf, D_MODEL), lambda i, f, m: (layer, m[i], ff(i, f, m), 0))],
        out_specs=pl.BlockSpec((tm, D_MODEL), lambda i, f, m: (i, 0)),
        scratch_shapes=[pltpu.VMEM((tm, D_MODEL), _BF16), pltpu.VMEM((tm, D_MODEL), _F32)],
    )
    return pl.pallas_call(
        _expert_kernel, grid_spec=grid_spec,
        out_shape=jax.ShapeDtypeStruct((P, D_MODEL), _F32),
        compiler_params=_cparams(("arbitrary", "arbitrary")),
        name="expert_ffn",
    )(meta, xg, wg, wu, wd)


def _combine_kernel(cur_ref, nxt_ref, y_hbm, gate_ref, x_ref, g_ref, b_ref, xo_ref, xbo_ref, buf, sem, *, tm):
    i = pl.program_id(0)
    slot = i % 2

    def issue(idx_ref, s):
        def body(r, c):
            for k in range(TOP_K):
                d = idx_ref[0, 0, TOP_K * r + k]
                pltpu.make_async_copy(y_hbm.at[pl.ds(d, 1), :], buf.at[s, k, pl.ds(r, 1), :],
                                      sem.at[s]).start(priority=k % 2)
            return c
        lax.fori_loop(0, tm, body, 0, unroll=8)

    @pl.when(i == 0)
    def _():
        issue(cur_ref, 0)

    @pl.when(i + 1 < pl.num_programs(0))
    def _():
        issue(nxt_ref, 1 - slot)

    pltpu.make_async_copy(buf.at[slot], buf.at[slot], sem.at[slot]).wait()
    gates = gate_ref[...]
    f = gates[:, 0:1] * buf[slot, 0] + gates[:, 1:2] * buf[slot, 1]
    y = _layer_norm(DEEPNORM_ALPHA * x_ref[...] + f, g_ref[...], b_ref[...])
    xo_ref[...] = y
    xbo_ref[...] = y.astype(xbo_ref.dtype)


def _combine(dest3, y, gates, x, g, b):
    T = x.shape[0]
    tm = dest3.shape[2] // TOP_K
    n = T // tm
    row = lambda i: (i, 0)
    full = lambda i: (0, 0)
    smem = lambda imap: pl.BlockSpec((1, 1, TOP_K * tm), imap, memory_space=pltpu.SMEM)
    return pl.pallas_call(
        functools.partial(_combine_kernel, tm=tm),
        grid=(n,),
        in_specs=[smem(lambda i: (i, 0, 0)), smem(lambda i: (jnp.minimum(i + 1, n - 1), 0, 0)),
                  pl.BlockSpec(memory_space=pl.ANY), pl.BlockSpec((tm, TOP_K), row),
                  pl.BlockSpec((tm, D_MODEL), row), pl.BlockSpec(g.shape, full), pl.BlockSpec(b.shape, full)],
        out_specs=[pl.BlockSpec((tm, D_MODEL), row), pl.BlockSpec((tm, D_MODEL), row)],
        out_shape=[jax.ShapeDtypeStruct((T, D_MODEL), _F32), jax.ShapeDtypeStruct((T, D_MODEL), _BF16)],
        scratch_shapes=[pltpu.VMEM((2, TOP_K, tm, D_MODEL), _F32), pltpu.SemaphoreType.DMA((2,))],
        compiler_params=_cparams(("arbitrary",)),
        name="moe_combine_ln",
    )(dest3, dest3, y, gates, x, g, b)


def _route(logits, T):
    eid = jnp.arange(N_EXPERTS, dtype=jnp.int32)[None, :]
    i1 = jnp.argmax(logits, axis=-1).astype(jnp.int32)
    v1 = jnp.max(logits, axis=-1)
    rest = jnp.where(eid == i1[:, None], -jnp.inf, logits)
    i2 = jnp.argmax(rest, axis=-1).astype(jnp.int32)
    v2 = jnp.max(rest, axis=-1)
    e2 = jnp.exp(v2 - v1)
    gates = jnp.stack([1.0 / (1.0 + e2), e2 / (1.0 + e2)], axis=-1)
    e = jnp.stack([i1, i2], axis=-1).reshape(T * TOP_K)
    onehot = (e[:, None] == eid).astype(jnp.int32)
    rank = jnp.sum((jnp.cumsum(onehot, axis=0) - onehot) * onehot, axis=-1)
    counts = jnp.sum(onehot, axis=0)
    padded = ((counts + EXPERT_ROWS - 1) // EXPERT_ROWS) * EXPERT_ROWS
    pend = jnp.cumsum(padded)
    dest = jnp.sum(onehot * (pend - padded)[None, :], axis=-1) + rank
    n_blk = (T * TOP_K) // EXPERT_ROWS + N_EXPERTS
    blk_start = jnp.arange(n_blk, dtype=jnp.int32) * EXPERT_ROWS
    blk_e = jnp.minimum(jnp.sum((blk_start[:, None] >= pend[None, :]).astype(jnp.int32), axis=-1), N_EXPERTS - 1)
    meta = jnp.concatenate([blk_e, pend[-1:] // EXPERT_ROWS]).astype(jnp.int32)
    return gates, dest.astype(jnp.int32), meta


def _moe(x, logits, wg, wu, wd, layer, g, b):
    T = x.shape[0]
    gates, dest, meta = _route(logits[:, :N_EXPERTS], T)
    tm = min(MOE_ROW_TILE, T)
    dest3 = dest.reshape(T // tm, 1, TOP_K * tm)
    P = T * TOP_K + N_EXPERTS * EXPERT_ROWS
    xg = _dispatch(dest3, x, P)
    y = _experts(meta, xg, wg, wu, wd, layer)
    return _combine(dest3, y, gates, x, g, b)


def kernel(x, positions, w_in, conv_w, a_log, dt_bias, gdn_norm_w, lam_q1, lam_k1, lam_q2, lam_k2, subln_w, w_out, ln1_g, ln1_b, ln2_g, ln2_b, ffn_w_gate, ffn_w_up, ffn_w_down, router_w, moe_w_gate, moe_w_up, moe_w_down):
    B, S, _ = x.shape
    T = B * S
    half = DA_HEAD_DIM // 2
    inv_freq = ROPE_THETA ** (-jnp.arange(0, DA_HEAD_DIM, 2, dtype=_F32) / DA_HEAD_DIM)
    ang = positions.astype(_F32).reshape(T, 1) * inv_freq[None, :]
    cos = jnp.tile(jnp.cos(ang), (1, HEAD_W // half))
    sin = jnp.tile(jnp.concatenate([-jnp.sin(ang), jnp.sin(ang)], axis=1), (1, HEAD_W // DA_HEAD_DIM))

    v0, v1 = 2 * GROUP_W, 3 * GROUP_W
    ab0 = 4 * GROUP_W + GDN_QKV_W
    xf = x.reshape(T, D_MODEL)
    xb = xf.astype(_BF16)
    row = lambda v: v.reshape(1, -1).astype(_F32)
    pad_lanes = lambda v: jnp.pad(v.astype(_F32), (0, HEAD_W - v.shape[0])).reshape(1, HEAD_W)
    w_in_b = w_in.astype(_BF16)
    w_out_b = w_out.astype(_BF16)
    moe_gate_b, moe_up_b, moe_down_b = (w.astype(_BF16) for w in (moe_w_gate, moe_w_up, moe_w_down))

    for l in range(DEPTH):
        lam_init = 0.8 - 0.6 * math.exp(-0.3 * l)
        wvt = w_in_b[l, :, v0:v1].T
        wab = jnp.pad(w_in_b[l, :, ab0:ab0 + 2 * N_HEADS], ((0, 0), (0, HEAD_W - 2 * N_HEADS)))
        q, k, vt, g, z, gb = _proj(xb, w_in_b, l, wvt, wab, cos, sin, pad_lanes(jnp.exp(a_log[l])),
                                   pad_lanes(dt_bias[l]), min(ATTN_TILE, S))
        lamv = jnp.stack([lam_q1[l], lam_k1[l], lam_q2[l], lam_k2[l]]).astype(_F32)
        da = _attention(q, k, vt, lamv, row(subln_w[l]), B, S, lam_init)
        gd = _gdn(g, conv_w[l].astype(_F32), gb, z, row(gdn_norm_w[l]), B, S)
        if l % 2 == 0:
            xf, xb = _outproj(da, gd, xf, w_out_b, l, row(ln1_g[l]), row(ln1_b[l]))
            i = l // 2
            xf, xb = _ffn(xb, xf, ffn_w_gate[i].astype(_BF16), ffn_w_up[i].astype(_BF16),
                          ffn_w_down[i].astype(_BF16), row(ln2_g[l]), row(ln2_b[l]))
        else:
            i = l // 2
            rw = jnp.pad(router_w[i].astype(_F32), ((0, 0), (0, HEAD_W - N_EXPERTS)))
            xf, xb, logits = _outproj(da, gd, xf, w_out_b, l, row(ln1_g[l]), row(ln1_b[l]), rw)
            xf, xb = _moe(xf, logits, moe_gate_b, moe_up_b, moe_down_b, i, row(ln2_g[l]), row(ln2_b[l]))
    return xf.reshape(B, S, D_MODEL)
```
